```python
import math
import jax, jax.numpy as jnp
from jax import lax
import numpy as np

D_MODEL = 4096
BATCH = 4
SEQ = 4096
DEPTH = 2

N_HEADS = 16
N_KV_HEADS = 4
HEAD_DIM = 128
D_ATTN = N_HEADS * HEAD_DIM
D_KV = N_KV_HEADS * HEAD_DIM
IDX_HEADS = 32
IDX_DIM = 64
MAX_TOPK = 256
Q_BLOCK = 128
ROPE_THETA = 10000.0
D_SSD = D_MODEL
SSD_HEAD_DIM = 64
SSD_HEADS = D_SSD // SSD_HEAD_DIM
SSD_GROUPS = 8
SSD_STATE = 128
CONV_WIDTH = 4
SSD_CHUNK = 128
CONV_CH = D_SSD + 2 * SSD_GROUPS * SSD_STATE
D_FF = 8192
EPS = 1e-6

SPLIT_SIZES = (D_ATTN, D_KV, D_KV, IDX_HEADS * IDX_DIM, IDX_DIM, IDX_HEADS,
               D_SSD, D_SSD, SSD_GROUPS * SSD_STATE, SSD_GROUPS * SSD_STATE, SSD_HEADS,
               D_MODEL, D_MODEL)
N_IN = (D_ATTN + 2 * D_KV + IDX_HEADS * IDX_DIM + IDX_DIM + IDX_HEADS
        + 2 * D_SSD + 2 * SSD_GROUPS * SSD_STATE + SSD_HEADS + 2 * D_MODEL)

kernel_name = "hybrid_dsa_ssd_macaron_sandwich"


def _split_points(sizes):
    return [int(s) for s in np.cumsum(sizes)[:-1]]


def rms_norm(x, g):
    xf = x.astype(jnp.float32)
    y = xf * lax.rsqrt(jnp.mean(xf * xf, axis=-1, keepdims=True) + EPS)
    return (y * g.astype(jnp.float32)).astype(x.dtype)


def layer_norm(x, g, b):
    xf = x.astype(jnp.float32)
    mu = jnp.mean(xf, axis=-1, keepdims=True)
    var = jnp.mean(jnp.square(xf - mu), axis=-1, keepdims=True)
    y = (xf - mu) * lax.rsqrt(var + EPS) * g.astype(jnp.float32) + b.astype(jnp.float32)
    return y.astype(x.dtype)


def rope(x, positions):
    d = x.shape[-1]
    half = d // 2
    inv_freq = 1.0 / (ROPE_THETA ** (jnp.arange(half, dtype=jnp.float32) * (2.0 / d)))
    ang = positions.astype(jnp.float32)[..., None] * inv_freq
    cos = jnp.cos(ang)[:, :, None, :]
    sin = jnp.sin(ang)[:, :, None, :]
    xf = x.astype(jnp.float32)
    x1, x2 = xf[..., :half], xf[..., half:]
    return jnp.concatenate([x1 * cos - x2 * sin, x2 * cos + x1 * sin], axis=-1).astype(x.dtype)


def swiglu(x, w13, w2):
    a, b = jnp.split(x @ w13, 2, axis=-1)
    return (jax.nn.silu(a) * b) @ w2


def causal_depthwise_conv(x, w, b):
    c = x.shape[-1]
    out = lax.conv_general_dilated(
        x, w[:, None, :].astype(x.dtype), window_strides=(1,),
        padding=[(CONV_WIDTH - 1, 0)], dimension_numbers=("NWC", "WIO", "NWC"),
        feature_group_count=c)
    return out + b


def dsa_attention(q, k, v, qi, ki, wi, topk):
    bsz, s_len = q.shape[0], q.shape[1]
    nb = s_len // Q_BLOCK
    grp = N_HEADS // N_KV_HEADS
    scale = HEAD_DIM ** -0.5
    key_pos = jnp.arange(s_len)
    bidx = jnp.arange(bsz)[:, None, None]

    def to_blocks(t):
        return t.reshape(bsz, nb, Q_BLOCK, *t.shape[2:]).swapaxes(0, 1)

    def block(args):
        qb, qib, wib, blk = args
        qpos = blk * Q_BLOCK + jnp.arange(Q_BLOCK)
        causal = key_pos[None, :] <= qpos[:, None]
        dots = jnp.einsum('bqhd,bsd->bqhs', qib, ki)
        score = jnp.einsum('bqhs,bqh->bqs', jax.nn.relu(dots).astype(jnp.float32),
                           wib.astype(jnp.float32))
        score = jnp.where(causal[None], score, -jnp.inf)
        _, sel = lax.top_k(score, topk)
        valid = sel <= qpos[None, :, None]
        ks = k[bidx, sel]
        vs = v[bidx, sel]
        qg = qb.reshape(bsz, Q_BLOCK, N_KV_HEADS, grp, HEAD_DIM)
        logits = jnp.einsum('bqhgd,bqkhd->bqhgk', qg, ks).astype(jnp.float32) * scale
        logits = jnp.where(valid[:, :, None, None, :], logits, -jnp.inf)
        p = jax.nn.softmax(logits, axis=-1).astype(v.dtype)
        o = jnp.einsum('bqhgk,bqkhd->bqhgd', p, vs)
        return o.reshape(bsz, Q_BLOCK, N_HEADS * HEAD_DIM)

    out = lax.map(block, (to_blocks(q), to_blocks(qi), to_blocks(wi), jnp.arange(nb)))
    return out.swapaxes(0, 1).reshape(bsz, s_len, N_HEADS * HEAD_DIM)


def ssd_scan(x, dt, a, bm, cm, d_skip):
    bsz, l_len, h, p = x.shape
    g, n = bm.shape[2], bm.shape[3]
    r = h // g
    nc = l_len // SSD_CHUNK
    cl = SSD_CHUNK
    xf = x.astype(jnp.float32)
    xdt = (xf * dt[..., None]).reshape(bsz, nc, cl, g, r, p)
    adt = (dt * a).reshape(bsz, nc, cl, g, r).transpose(0, 3, 4, 1, 2)
    a_cs = jnp.cumsum(adt, axis=-1)
    bc = bm.astype(jnp.float32).reshape(bsz, nc, cl, g, n)
    cc = cm.astype(jnp.float32).reshape(bsz, nc, cl, g, n)
    tri = jnp.tril(jnp.ones((cl, cl), dtype=bool))
    seg = a_cs[..., :, None] - a_cs[..., None, :]
    lmat = jnp.exp(jnp.where(tri, seg, -jnp.inf))
    cb = jnp.einsum('bclgn,bcsgn->bgcls', cc, bc)
    y_diag = jnp.einsum('bgrcls,bcsgrp->bclgrp', cb[:, :, None] * lmat, xdt)
    decay_states = jnp.exp(a_cs[..., -1:] - a_cs).transpose(0, 3, 4, 1, 2)
    states = jnp.einsum('bclgn,bclgrp->bcgrpn', bc, xdt * decay_states[..., None])
    chunk_decay = jnp.exp(a_cs[..., -1])

    def step(hs, inp):
        s_c, d_c = inp
        return hs * d_c[..., None, None] + s_c, hs

    h0 = jnp.zeros((bsz, g, r, p, n), jnp.float32)
    _, prev = lax.scan(step, h0, (jnp.moveaxis(states, 1, 0), jnp.moveaxis(chunk_decay, -1, 0)))
    prev = jnp.moveaxis(prev, 0, 1)
    decay_out = jnp.exp(a_cs).transpose(0, 3, 4, 1, 2)
    y_off = jnp.einsum('bclgn,bcgrpn->bclgrp', cc, prev) * decay_out[..., None]
    y = (y_diag + y_off).reshape(bsz, l_len, h, p) + xf * d_skip.astype(jnp.float32)[:, None]
    return y.astype(x.dtype)


def hybrid_mixer(hn, positions, w_in, idx_ln_g, idx_ln_b, conv_w, conv_b, dt_bias, a_log,
                 d_skip, ssd_norm_g, w_attn_o, w_ssd_o, w_out):
    bsz, s_len, _ = hn.shape
    proj = hn @ w_in
    (q, k, v, qi, ki, wi, z, xs, bs, cs, dt, g_attn, g_ssd) = jnp.split(
        proj, _split_points(SPLIT_SIZES), axis=-1)
    q = rope(q.reshape(bsz, s_len, N_HEADS, HEAD_DIM), positions)
    k = rope(k.reshape(bsz, s_len, N_KV_HEADS, HEAD_DIM), positions)
    v = v.reshape(bsz, s_len, N_KV_HEADS, HEAD_DIM)
    qi = rope(qi.reshape(bsz, s_len, IDX_HEADS, IDX_DIM), positions)
    ki = rope(layer_norm(ki, idx_ln_g, idx_ln_b)[:, :, None, :], positions)[:, :, 0]
    wi = wi * (IDX_HEADS ** -0.5 * IDX_DIM ** -0.5)
    topk = min(MAX_TOPK, s_len // 4)
    o_attn = dsa_attention(q, k, v, qi, ki, wi, topk)
    xbc = jax.nn.silu(causal_depthwise_conv(jnp.concatenate([xs, bs, cs], axis=-1), conv_w, conv_b))
    xs, bs, cs = jnp.split(xbc, _split_points((D_SSD, SSD_GROUPS * SSD_STATE, SSD_GROUPS * SSD_STATE)), axis=-1)
    dt = jax.nn.softplus(dt.astype(jnp.float32) + dt_bias.astype(jnp.float32))
    a = -jnp.exp(a_log.astype(jnp.float32))
    y = ssd_scan(xs.reshape(bsz, s_len, SSD_HEADS, SSD_HEAD_DIM), dt, a,
                 bs.reshape(bsz, s_len, SSD_GROUPS, SSD_STATE),
                 cs.reshape(bsz, s_len, SSD_GROUPS, SSD_STATE), d_skip)
    y = y.reshape(bsz, s_len, D_SSD) * jax.nn.silu(z)
    y = rms_norm(y.reshape(bsz, s_len, SSD_GROUPS, D_SSD // SSD_GROUPS),
                 ssd_norm_g.reshape(SSD_GROUPS, D_SSD // SSD_GROUPS)).reshape(bsz, s_len, D_SSD)
    merged = jax.nn.sigmoid(g_attn) * (o_attn @ w_attn_o) + jax.nn.sigmoid(g_ssd) * (y @ w_ssd_o)
    return merged @ w_out


def setup_inputs(seed: int = 0) -> dict:
    key = jax.random.key(seed)
    ks = jax.random.split(key, 26)
    L, D = DEPTH, D_MODEL

    def normal(k, shape, scale):
        return jax.random.normal(k, shape, jnp.float32) * scale

    def gain(k, shape):
        return 1.0 + 0.02 * jax.random.normal(k, shape, jnp.float32)

    dt0 = jnp.exp(jax.random.uniform(ks[11], (L, SSD_HEADS), jnp.float32,
                                     math.log(1e-3), math.log(1e-1)))
    return {
        "x": normal(ks[0], (BATCH, SEQ, D), 1.0),
        "positions": jnp.broadcast_to(jnp.arange(SEQ, dtype=jnp.int32)[None, :], (BATCH, SEQ)),
        "ffn1_pre_g": gain(ks[1], (L, D)),
        "ffn1_w13": normal(ks[2], (L, D, 2 * D_FF), D ** -0.5),
        "ffn1_w2": normal(ks[3], (L, D_FF, D), D_FF ** -0.5),
        "ffn1_post_g": gain(ks[4], (L, D)),
        "mix_pre_g": gain(ks[5], (L, D)),
        "w_in": normal(ks[6], (L, D, N_IN), D ** -0.5),
        "idx_ln_g": gain(ks[7], (L, IDX_DIM)),
        "idx_ln_b": normal(ks[8], (L, IDX_DIM), 0.01),
        "conv_w": normal(ks[9], (L, CONV_WIDTH, CONV_CH), CONV_WIDTH ** -0.5),
        "conv_b": normal(ks[10], (L, CONV_CH), 0.01),
        "dt_bias": dt0 + jnp.log(-jnp.expm1(-dt0)),
        "a_log": jnp.log(jax.random.uniform(ks[12], (L, SSD_HEADS), jnp.float32, 1.0, 16.0)),
        "d_skip": gain(ks[13], (L, SSD_HEADS)),
        "ssd_norm_g": gain(ks[14], (L, D_SSD)),
        "w_attn_o": normal(ks[15], (L, D_ATTN, D), D_ATTN ** -0.5),
        "w_ssd_o": normal(ks[16], (L, D_SSD, D), D_SSD ** -0.5),
        "w_out": normal(ks[17], (L, D, D), D ** -0.5),
        "mix_post_g": gain(ks[18], (L, D)),
        "ffn2_pre_g": gain(ks[19], (L, D)),
        "ffn2_w13": normal(ks[20], (L, D, 2 * D_FF), D ** -0.5),
        "ffn2_w2": normal(ks[21], (L, D_FF, D), D_FF ** -0.5),
        "ffn2_post_g": gain(ks[22], (L, D)),
    }


def reference(x, positions, ffn1_pre_g, ffn1_w13, ffn1_w2, ffn1_post_g, mix_pre_g, w_in,
              idx_ln_g, idx_ln_b, conv_w, conv_b, dt_bias, a_log, d_skip, ssd_norm_g,
              w_attn_o, w_ssd_o, w_out, mix_post_g, ffn2_pre_g, ffn2_w13, ffn2_w2, ffn2_post_g):
    for l in range(DEPTH):
        x = x + 0.5 * rms_norm(swiglu(rms_norm(x, ffn1_pre_g[l]), ffn1_w13[l], ffn1_w2[l]),
                               ffn1_post_g[l])
        m = hybrid_mixer(rms_norm(x, mix_pre_g[l]), positions, w_in[l], idx_ln_g[l], idx_ln_b[l],
                         conv_w[l], conv_b[l], dt_bias[l], a_log[l], d_skip[l], ssd_norm_g[l],
                         w_attn_o[l], w_ssd_o[l], w_out[l])
        x = x + rms_norm(m, mix_post_g[l])
        x = x + 0.5 * rms_norm(swiglu(rms_norm(x, ffn2_pre_g[l]), ffn2_w13[l], ffn2_w2[l]),
                               ffn2_post_g[l])
    return x
```

```python
import functools

import jax
import jax.numpy as jnp
import numpy as np
from jax import lax
from jax.experimental import pallas as pl
from jax.experimental.pallas import tpu as pltpu

N_HEADS = 16
N_KV_HEADS = 4
HEAD_DIM = 128
GQA = N_HEADS // N_KV_HEADS
D_ATTN = N_HEADS * HEAD_DIM
D_KV = N_KV_HEADS * HEAD_DIM
IDX_HEADS = 32
IDX_DIM = 64
D_IDX = IDX_HEADS * IDX_DIM
MAX_TOPK = 256
Q_BLOCK = 128
ROPE_THETA = 10000.0
SSD_HEAD_DIM = 64
SSD_GROUPS = 8
SSD_STATE = 128
CONV_WIDTH = 4
SSD_CHUNK = 128
EPS = 1e-6

LANES = 128
SUBLANES = 8
VMEM_LIMIT = 56 * 1024 * 1024

KEY_BLOCK = 256
NEG_BIG = -1e30
INT_MIN = -(2 ** 31)

BF16 = jnp.bfloat16
F32 = jnp.float32


def _cparams(sem):
    return pltpu.CompilerParams(dimension_semantics=sem, vmem_limit_bytes=VMEM_LIMIT)


def _pick(n, pref):
    t = min(pref, n)
    while n % t:
        t //= 2
    return t


def _rms(v, g):
    return v * lax.rsqrt(jnp.mean(v * v, axis=-1, keepdims=True) + EPS) * g


def _prenorm_kernel(x_ref, g_ref, o_ref):
    o_ref[...] = _rms(x_ref[...], g_ref[...]).astype(o_ref.dtype)


def _prenorm(x, g):
    m, d = x.shape
    tr = _pick(m, 256)
    return pl.pallas_call(
        _prenorm_kernel,
        grid=(m // tr,),
        in_specs=[pl.BlockSpec((tr, d), lambda i: (i, 0)), pl.BlockSpec((1, d), lambda i: (0, 0))],
        out_specs=pl.BlockSpec((tr, d), lambda i: (i, 0)),
        out_shape=jax.ShapeDtypeStruct((m, d), BF16),
        compiler_params=_cparams(("parallel",)),
        name="prenorm",
    )(x, g.reshape(1, d))


def _post_kernel(m_ref, x_ref, gp_ref, gn_ref, xo_ref, hn_ref, *, coef):
    xn = x_ref[...] + coef * _rms(m_ref[...], gp_ref[...])
    xo_ref[...] = xn
    hn_ref[...] = _rms(xn, gn_ref[...]).astype(hn_ref.dtype)


def _post_last_kernel(m_ref, x_ref, gp_ref, xo_ref, *, coef):
    xo_ref[...] = x_ref[...] + coef * _rms(m_ref[...], gp_ref[...])


def _post(mm, x, g_post, g_next, coef):
    m, d = x.shape
    tr = _pick(m, 256)
    row = pl.BlockSpec((tr, d), lambda i: (i, 0))
    vec = pl.BlockSpec((1, d), lambda i: (0, 0))
    if g_next is None:
        return pl.pallas_call(
            functools.partial(_post_last_kernel, coef=coef),
            grid=(m // tr,),
            in_specs=[row, row, vec],
            out_specs=row,
            out_shape=jax.ShapeDtypeStruct((m, d), F32),
            compiler_params=_cparams(("parallel",)),
            name="post_last",
        )(mm, x, g_post.reshape(1, d)), None
    return pl.pallas_call(
        functools.partial(_post_kernel, coef=coef),
        grid=(m // tr,),
        in_specs=[row, row, vec, vec],
        out_specs=[row, row],
        out_shape=[jax.ShapeDtypeStruct((m, d), F32), jax.ShapeDtypeStruct((m, d), BF16)],
        compiler_params=_cparams(("parallel",)),
        name="post",
    )(mm, x, g_post.reshape(1, d), g_next.reshape(1, d))


def _mm_kernel(x_ref, w_ref, o_ref, *scratch, nk):
    part = jnp.dot(x_ref[...], w_ref[...], preferred_element_type=F32)
    if nk == 1:
        o_ref[...] = part.astype(o_ref.dtype)
        return
    acc_ref, = scratch
    k = pl.program_id(2)

    @pl.when(k == 0)
    def _():
        acc_ref[...] = part

    @pl.when(k > 0)
    def _():
        acc_ref[...] += part

    @pl.when(k == nk - 1)
    def _():
        o_ref[...] = acc_ref[...].astype(o_ref.dtype)


def _matmul(x, w, out_dtype, tm=1024, tn=1024, tk=4096, name="matmul"):
    m, kd = x.shape
    n = w.shape[1]
    tm, tn, tk = _pick(m, tm), _pick(n, tn), _pick(kd, tk)
    nk = kd // tk
    return pl.pallas_call(
        functools.partial(_mm_kernel, nk=nk),
        grid=(m // tm, n // tn, nk),
        in_specs=[pl.BlockSpec((tm, tk), lambda i, j, k: (i, k)),
                  pl.BlockSpec((tk, tn), lambda i, j, k: (k, j))],
        out_specs=pl.BlockSpec((tm, tn), lambda i, j, k: (i, j)),
        out_shape=jax.ShapeDtypeStruct((m, n), out_dtype),
        scratch_shapes=[pltpu.VMEM((tm, tn), F32)] if nk > 1 else [],
        compiler_params=_cparams(("parallel", "parallel", "arbitrary")),
        name=name,
    )(x, w)


def _swiglu_kernel(x_ref, w1_ref, w3_ref, o_ref):
    x = x_ref[...]
    a = jnp.dot(x, w1_ref[...], preferred_element_type=F32)
    b = jnp.dot(x, w3_ref[...], preferred_element_type=F32)
    o_ref[...] = (a * jax.nn.sigmoid(a) * b).astype(o_ref.dtype)


def _swiglu_up(hn, w13):
    m, kd = hn.shape
    f = w13.shape[1] // 2
    tm, tn = _pick(m, 1024), _pick(f, 512)
    nj = f // tn
    return pl.pallas_call(
        _swiglu_kernel,
        grid=(m // tm, nj),
        in_specs=[pl.BlockSpec((tm, kd), lambda i, j: (i, 0)),
                  pl.BlockSpec((kd, tn), lambda i, j: (0, j)),
                  pl.BlockSpec((kd, tn), lambda i, j: (0, j + nj))],
        out_specs=pl.BlockSpec((tm, tn), lambda i, j: (i, j)),
        out_shape=jax.ShapeDtypeStruct((m, f), BF16),
        compiler_params=_cparams(("parallel", "parallel")),
        name="swiglu_up",
    )(hn, w13, w13)


def _merge_kernel(o_ref, wa_ref, y_ref, ws_ref, ga_ref, gs_ref, out_ref):
    a = jnp.dot(o_ref[...], wa_ref[...], preferred_element_type=F32)
    s = jnp.dot(y_ref[...], ws_ref[...], preferred_element_type=F32)
    ga = jax.nn.sigmoid(ga_ref[...].astype(F32))
    gs = jax.nn.sigmoid(gs_ref[...].astype(F32))
    out_ref[...] = (ga * a + gs * s).astype(out_ref.dtype)


def _merge(o_attn, wa, y, ws, proj, ga_off, gs_off, d):
    m = o_attn.shape[0]
    tm, tn = _pick(m, 1024), _pick(d, 512)
    ja, js = ga_off // tn, gs_off // tn
    return pl.pallas_call(
        _merge_kernel,
        grid=(m // tm, d // tn),
        in_specs=[pl.BlockSpec((tm, o_attn.shape[1]), lambda i, j: (i, 0)),
                  pl.BlockSpec((wa.shape[0], tn), lambda i, j: (0, j)),
                  pl.BlockSpec((tm, y.shape[1]), lambda i, j: (i, 0)),
                  pl.BlockSpec((ws.shape[0], tn), lambda i, j: (0, j)),
                  pl.BlockSpec((tm, tn), lambda i, j: (i, ja + j)),
                  pl.BlockSpec((tm, tn), lambda i, j: (i, js + j))],
        out_specs=pl.BlockSpec((tm, tn), lambda i, j: (i, j)),
        out_shape=jax.ShapeDtypeStruct((m, d), BF16),
        compiler_params=_cparams(("parallel", "parallel")),
        name="merge",
    )(o_attn, wa, y, ws, proj, proj)


def _rope_tables(pos, invf, sign):
    ang = pos * invf
    return jnp.cos(ang), jnp.sin(ang) * sign


def _rope128(x, cos, sin_signed):
    return x * cos + pltpu.roll(x, HEAD_DIM // 2, 1) * sin_signed


def _rope64(x, cos, sin_signed, first_half):
    half = IDX_DIM // 2
    rot = jnp.where(first_half, pltpu.roll(x, LANES - half, 1), pltpu.roll(x, half, 1))
    return x * cos + rot * sin_signed


def _attn_prep_kernel(pos_ref, q_ref, qi_ref, k_ref, sm_ref, f128_ref, s128_ref, f64_ref, s64_ref,
                      lng_ref, lnb_ref, qs_ref, qis_ref, kr_ref, kid_ref):
    pos = pos_ref[...].astype(F32)
    cos_a, sin_a = _rope_tables(pos, f128_ref[...], s128_ref[...])
    cos_b, sin_b = _rope_tables(pos, f64_ref[...], s64_ref[...])
    lane = lax.broadcasted_iota(jnp.int32, (Q_BLOCK, LANES), 1)
    first_half = (lane & (IDX_DIM - 1)) < IDX_DIM // 2
    low_head = lane < IDX_DIM
    scale = HEAD_DIM ** -0.5
    for h in range(N_HEADS):
        xh = q_ref[:, h * HEAD_DIM:(h + 1) * HEAD_DIM].astype(F32)
        r = _rope128(xh, cos_a, sin_a) * scale
        hk, g = divmod(h, GQA)
        qs_ref[0, hk, g * Q_BLOCK:(g + 1) * Q_BLOCK, :] = r.astype(qs_ref.dtype)
    for h in range(N_KV_HEADS):
        xh = k_ref[:, h * HEAD_DIM:(h + 1) * HEAD_DIM].astype(F32)
        kr_ref[:, h * HEAD_DIM:(h + 1) * HEAD_DIM] = _rope128(xh, cos_a, sin_a).astype(kr_ref.dtype)
    zero = jnp.zeros((Q_BLOCK, LANES), F32)
    for j in range(IDX_HEADS // 2):
        xp = qi_ref[:, j * LANES:(j + 1) * LANES].astype(F32)
        r = _rope64(xp, cos_b, sin_b, first_half)
        qis_ref[0, 2 * j] = jnp.where(low_head, r, zero).astype(qis_ref.dtype)
        qis_ref[0, 2 * j + 1] = jnp.where(low_head, zero, r).astype(qis_ref.dtype)
    ki = sm_ref[:, 0:LANES]
    mu = jnp.mean(ki, axis=-1, keepdims=True)
    var = jnp.mean(jnp.square(ki - mu), axis=-1, keepdims=True)
    kn = (ki - mu) * lax.rsqrt(var + EPS) * lng_ref[...] + lnb_ref[...]
    kid_ref[...] = _rope64(kn, cos_b, sin_b, first_half).astype(kid_ref.dtype)


def _attn_prep(pos, proj, small, offs, idx_ln_g, idx_ln_b):
    m = proj.shape[0]
    nblk = m // Q_BLOCK
    half_a, half_b = HEAD_DIM // 2, IDX_DIM // 2
    inv_a = 1.0 / (ROPE_THETA ** (jnp.arange(half_a, dtype=F32) * (2.0 / HEAD_DIM)))
    inv_b = 1.0 / (ROPE_THETA ** (jnp.arange(half_b, dtype=F32) * (2.0 / IDX_DIM)))
    f128 = jnp.tile(inv_a, 2).reshape(1, LANES)
    f64 = jnp.tile(inv_b, 4).reshape(1, LANES)
    s128 = jnp.asarray(np.repeat([-1.0, 1.0], half_a), F32).reshape(1, LANES)
    s64 = jnp.asarray(np.tile(np.repeat([-1.0, 1.0], half_b), 2), F32).reshape(1, LANES)
    lng = jnp.tile(idx_ln_g, 2).reshape(1, LANES)
    lnb = jnp.tile(idx_ln_b, 2).reshape(1, LANES)
    vec = pl.BlockSpec((1, LANES), lambda i: (0, 0))
    return pl.pallas_call(
        _attn_prep_kernel,
        grid=(nblk,),
        in_specs=[pl.BlockSpec((Q_BLOCK, 1), lambda i: (i, 0)),
                  pl.BlockSpec((Q_BLOCK, D_ATTN), lambda i: (i, offs["q"] // D_ATTN)),
                  pl.BlockSpec((Q_BLOCK, D_IDX), lambda i: (i, offs["qi"] // D_IDX)),
                  pl.BlockSpec((Q_BLOCK, D_KV), lambda i: (i, offs["k"] // D_KV)),
                  pl.BlockSpec((Q_BLOCK, small.shape[1]), lambda i: (i, 0)),
                  vec, vec, vec, vec, vec, vec],
        out_specs=[pl.BlockSpec((1, N_KV_HEADS, GQA * Q_BLOCK, HEAD_DIM), lambda i: (i, 0, 0, 0)),
                   pl.BlockSpec((1, IDX_HEADS, Q_BLOCK, LANES), lambda i: (i, 0, 0, 0)),
                   pl.BlockSpec((Q_BLOCK, D_KV), lambda i: (i, 0)),
                   pl.BlockSpec((Q_BLOCK, LANES), lambda i: (i, 0))],
        out_shape=[jax.ShapeDtypeStruct((nblk, N_KV_HEADS, GQA * Q_BLOCK, HEAD_DIM), BF16),
                   jax.ShapeDtypeStruct((nblk, IDX_HEADS, Q_BLOCK, LANES), BF16),
                   jax.ShapeDtypeStruct((m, D_KV), BF16),
                   jax.ShapeDtypeStruct((m, LANES), BF16)],
        compiler_params=_cparams(("parallel",)),
        name="attn_prep",
    )(pos, proj, proj, proj, small, f128, s128, f64, s64, lng, lnb)


def _dsa_kernel(qs_ref, qis_ref, sm_ref, kid_ref, kr_ref, v_ref, o_ref,
                dots_scr, wb_scr, key_scr, bias_scr, m_scr, l_scr, acc_scr, *, topk, wi_off):
    i = pl.program_id(1)
    nkb = (i * Q_BLOCK + Q_BLOCK + KEY_BLOCK - 1) // KEY_BLOCK
    nsub = KEY_BLOCK // LANES
    wi_scale = IDX_HEADS ** -0.5 * IDX_DIM ** -0.5

    for h in range(IDX_HEADS):
        col = sm_ref[:, wi_off + h:wi_off + h + 1] * wi_scale
        wb_scr[h] = jnp.broadcast_to(col, (Q_BLOCK, LANES))

    row_pos = i * Q_BLOCK + lax.broadcasted_iota(jnp.int32, (Q_BLOCK, KEY_BLOCK), 0)
    col_iota = lax.broadcasted_iota(jnp.int32, (Q_BLOCK, KEY_BLOCK), 1)

    def score_block(kb, carry):
        k0 = pl.multiple_of(kb * KEY_BLOCK, KEY_BLOCK)
        ki = kid_ref[pl.ds(k0, KEY_BLOCK), :]
        q2 = qis_ref[0].reshape(IDX_HEADS * Q_BLOCK, LANES)
        dots_scr[...] = lax.dot_general(q2, ki, (((1,), (1,)), ((), ())),
                                        preferred_element_type=F32).reshape(IDX_HEADS, Q_BLOCK, KEY_BLOCK)

        def head(h, acc):
            d = jnp.maximum(dots_scr[h], 0.0)
            w = wb_scr[h]
            return acc + d * jnp.concatenate([w] * nsub, axis=1)

        score = lax.fori_loop(0, IDX_HEADS, head, jnp.zeros((Q_BLOCK, KEY_BLOCK), F32))
        bits = lax.bitcast_convert_type(score, jnp.int32)
        skey = bits ^ ((bits >> 31) & jnp.int32(0x7FFFFFFF))
        causal = (k0 + col_iota) <= row_pos
        key_scr[kb] = jnp.where(causal, skey, jnp.int32(INT_MIN))
        return carry

    lax.fori_loop(0, nkb, score_block, 0)

    def bit_step(it, thr):
        cand = thr ^ lax.shift_left(jnp.int32(1), 31 - it)
        cand_b = jnp.broadcast_to(cand, (Q_BLOCK, LANES))

        def count_block(kb, acc):
            blk = key_scr[kb]
            for c in range(nsub):
                acc = acc + jnp.where(blk[:, c * LANES:(c + 1) * LANES] >= cand_b, 1, 0)
            return acc

        cnt = lax.fori_loop(0, nkb, count_block, jnp.zeros((Q_BLOCK, LANES), jnp.int32))
        total = jnp.sum(cnt, axis=1, keepdims=True)
        return jnp.where(total >= topk, cand, thr)

    thr = lax.fori_loop(0, 32, bit_step, jnp.full((Q_BLOCK, 1), INT_MIN, jnp.int32))
    thr = jnp.maximum(thr, jnp.int32(INT_MIN + 1))
    thr_b = jnp.broadcast_to(thr, (Q_BLOCK, KEY_BLOCK))

    def bias_block(kb, carry):
        bias_scr[kb] = jnp.where(key_scr[kb] >= thr_b, 0.0, NEG_BIG).astype(F32)
        return carry

    lax.fori_loop(0, nkb, bias_block, 0)

    for hk in range(N_KV_HEADS):
        m_scr[...] = jnp.full(m_scr.shape, NEG_BIG, F32)
        l_scr[...] = jnp.zeros(l_scr.shape, F32)
        acc_scr[...] = jnp.zeros(acc_scr.shape, F32)
        q = qs_ref[0, hk]

        def attn_block(kb, carry):
            k0 = pl.multiple_of(kb * KEY_BLOCK, KEY_BLOCK)
            kblk = kr_ref[pl.ds(k0, KEY_BLOCK), hk * HEAD_DIM:(hk + 1) * HEAD_DIM]
            vblk = v_ref[pl.ds(k0, KEY_BLOCK), hk * HEAD_DIM:(hk + 1) * HEAD_DIM]
            s = lax.dot_general(q, kblk, (((1,), (1,)), ((), ())), preferred_element_type=F32)
            b = bias_scr[kb]
            s = s + jnp.concatenate([b] * GQA, axis=0)
            m_old = m_scr[...]
            m_new = jnp.maximum(m_old, jnp.max(s, axis=1, keepdims=True))
            alpha = jnp.exp(m_old - m_new)
            p = jnp.exp(s - m_new)
            l_scr[...] = alpha * l_scr[...] + jnp.sum(p, axis=1, keepdims=True)
            acc_scr[...] = alpha * acc_scr[...] + jnp.dot(p.astype(BF16), vblk, preferred_element_type=F32)
            m_scr[...] = m_new
            return carry

        lax.fori_loop(0, nkb, attn_block, 0)
        out = acc_scr[...] / l_scr[...]
        for g in range(GQA):
            h = hk * GQA + g
            o_ref[:, h * HEAD_DIM:(h + 1) * HEAD_DIM] = out[g * Q_BLOCK:(g + 1) * Q_BLOCK].astype(o_ref.dtype)


def _dsa(qs, qis, small, kid, kr, proj, offs, bsz, s_len, wi_off):
    nblk = s_len // Q_BLOCK
    nkb = s_len // KEY_BLOCK
    topk = min(MAX_TOPK, s_len // 4)
    vcol = offs["v"] // D_KV
    return pl.pallas_call(
        functools.partial(_dsa_kernel, topk=topk, wi_off=wi_off),
        grid=(bsz, nblk),
        in_specs=[pl.BlockSpec((1, N_KV_HEADS, GQA * Q_BLOCK, HEAD_DIM), lambda b, i: (b * nblk + i, 0, 0, 0)),
                  pl.BlockSpec((1, IDX_HEADS, Q_BLOCK, LANES), lambda b, i: (b * nblk + i, 0, 0, 0)),
                  pl.BlockSpec((Q_BLOCK, small.shape[1]), lambda b, i: (b * nblk + i, 0)),
                  pl.BlockSpec((s_len, LANES), lambda b, i: (b, 0)),
                  pl.BlockSpec((s_len, D_KV), lambda b, i: (b, 0)),
                  pl.BlockSpec((s_len, D_KV), lambda b, i: (b, vcol))],
        out_specs=pl.BlockSpec((Q_BLOCK, D_ATTN), lambda b, i: (b * nblk + i, 0)),
        out_shape=jax.ShapeDtypeStruct((bsz * s_len, D_ATTN), BF16),
        scratch_shapes=[pltpu.VMEM((IDX_HEADS, Q_BLOCK, KEY_BLOCK), F32),
                        pltpu.VMEM((IDX_HEADS, Q_BLOCK, LANES), F32),
                        pltpu.VMEM((nkb, Q_BLOCK, KEY_BLOCK), jnp.int32),
                        pltpu.VMEM((nkb, Q_BLOCK, KEY_BLOCK), F32),
                        pltpu.VMEM((GQA * Q_BLOCK, 1), F32),
                        pltpu.VMEM((GQA * Q_BLOCK, 1), F32),
                        pltpu.VMEM((GQA * Q_BLOCK, HEAD_DIM), F32)],
        compiler_params=_cparams(("parallel", "arbitrary")),
        name="dsa",
    )(qs, qis, small, kid, kr, proj)


def _conv_kernel(x_ref, halo_ref, w_ref, b_ref, o_ref, *, tiles_per_seq):
    st = x_ref.shape[0]
    first = (pl.program_id(0) % tiles_per_seq) == 0
    halo = jnp.where(first, 0.0, halo_ref[...].astype(F32))
    xx = jnp.concatenate([halo, x_ref[...].astype(F32)], axis=0)
    acc = b_ref[...] + w_ref[CONV_WIDTH - 1:CONV_WIDTH, :] * xx[SUBLANES:, :]
    for k in range(CONV_WIDTH - 1):
        shifted = pltpu.roll(xx, CONV_WIDTH - 1 - k, 0)[SUBLANES:, :]
        acc = acc + w_ref[k:k + 1, :] * shifted
    o_ref[...] = (acc * jax.nn.sigmoid(acc)).astype(o_ref.dtype)


def _conv(proj, xs_off, conv_w, conv_b, s_len):
    m = proj.shape[0]
    ch = conv_w.shape[1]
    st, ct = _pick(s_len, 512), _pick(ch, 512)
    c0 = xs_off // ct
    hb = st // SUBLANES
    return pl.pallas_call(
        functools.partial(_conv_kernel, tiles_per_seq=s_len // st),
        grid=(m // st, ch // ct),
        in_specs=[pl.BlockSpec((st, ct), lambda r, c: (r, c0 + c)),
                  pl.BlockSpec((SUBLANES, ct), lambda r, c: (jnp.maximum(r * hb - 1, 0), c0 + c)),
                  pl.BlockSpec((CONV_WIDTH, ct), lambda r, c: (0, c)),
                  pl.BlockSpec((1, ct), lambda r, c: (0, c))],
        out_specs=pl.BlockSpec((st, ct), lambda r, c: (r, c)),
        out_shape=jax.ShapeDtypeStruct((m, ch), BF16),
        compiler_params=_cparams(("parallel", "parallel")),
        name="conv",
    )(proj, proj, conv_w, conv_b.reshape(1, ch))


def _softplus(x):
    return jnp.maximum(x, 0.0) + jnp.log1p(jnp.exp(-jnp.abs(x)))


def _ssd_kernel(x_ref, b_ref, c_ref, z_ref, dtc_ref, dtr_ref, pc_ref, pr_ref, dsk_ref, ng_ref, o_ref,
                state_scr, *, heads):
    gw = heads * SSD_HEAD_DIM
    rep = LANES // heads
    hi = lax.Precision.HIGHEST
    cl = SSD_CHUNK

    @pl.when(pl.program_id(2) == 0)
    def _():
        state_scr[...] = jnp.zeros(state_scr.shape, F32)

    dtc = _softplus(dtc_ref[0, 0] + pc_ref[0, 0:1, :])
    dtr = _softplus(dtr_ref[0, 0] + pr_ref[0, :, 0:1])
    adt_c = dtc * -jnp.exp(pc_ref[0, 1:2, :])
    adt_r = dtr * -jnp.exp(pr_ref[0, :, 1:2])
    li = lax.broadcasted_iota(jnp.int32, (cl, cl), 0)
    si = lax.broadcasted_iota(jnp.int32, (cl, cl), 1)
    tri = li >= si
    tri_f = tri.astype(F32)
    acs_c = jnp.dot(tri_f, adt_c, preferred_element_type=F32, precision=hi)
    acs_r = lax.dot_general(adt_r, tri_f, (((1,), (1,)), ((), ())), preferred_element_type=F32, precision=hi)
    last = acs_c[cl - 1:cl, :]
    er = lax.broadcasted_iota(jnp.int32, (LANES, gw), 0)
    ec = lax.broadcasted_iota(jnp.int32, (LANES, gw), 1)
    expand = (er == lax.shift_right_logical(ec, SSD_HEAD_DIM.bit_length() - 1) * rep).astype(F32)
    stacked = jnp.concatenate([dtc, dtc * jnp.exp(last - acs_c), jnp.exp(acs_c)], axis=0)
    ex = jnp.dot(stacked, expand, preferred_element_type=F32, precision=hi)
    dt_e, dts_e, dec_e = ex[0:cl], ex[cl:2 * cl], ex[2 * cl:3 * cl]

    xg = x_ref[...].astype(F32)
    xdt = (xg * dt_e).astype(BF16)
    xst = (xg * dts_e).astype(BF16)
    bm = b_ref[...]
    cm = c_ref[...]
    cb = lax.dot_general(cm, bm, (((1,), (1,)), ((), ())), preferred_element_type=F32)
    lane = lax.broadcasted_iota(jnp.int32, (cl, LANES), 1)
    pieces = []
    for j in range(gw // LANES):
        ys = []
        for r in (2 * j, 2 * j + 1):
            col = jnp.broadcast_to(acs_c[:, r * rep:r * rep + 1], (cl, cl))
            row = jnp.broadcast_to(acs_r[r * rep:r * rep + 1, :], (cl, cl))
            lmat = jnp.where(tri, jnp.exp(col - row), 0.0)
            ys.append(jnp.dot((cb * lmat).astype(BF16), xdt[:, j * LANES:(j + 1) * LANES],
                              preferred_element_type=F32))
        pieces.append(jnp.where(lane < SSD_HEAD_DIM, ys[0], ys[1]))
    y_diag = jnp.concatenate(pieces, axis=1) if len(pieces) > 1 else pieces[0]

    state = state_scr[...]
    y_off = jnp.dot(cm, state.astype(BF16), preferred_element_type=F32) * dec_e
    y = y_diag + y_off + xg * dsk_ref[0]
    state_scr[...] = state * dec_e[cl - 1:cl, :] + lax.dot_general(
        bm, xst, (((0,), (0,)), ((), ())), preferred_element_type=F32)

    z = z_ref[...].astype(F32)
    yz = y * (z * jax.nn.sigmoid(z))
    o_ref[...] = _rms(yz, ng_ref[...]).astype(o_ref.dtype)


def _ssd(conv_out, proj, small, z_off, dt_off, dt_bias, a_log, d_skip, norm_g, bsz, s_len):
    d_ssd = norm_g.shape[0]
    n_heads = a_log.shape[0]
    heads = n_heads // SSD_GROUPS
    gw = heads * SSD_HEAD_DIM
    rep = LANES // heads
    nc = s_len // SSD_CHUNK
    m = bsz * s_len
    dt = small[:, dt_off:dt_off + n_heads].reshape(bsz, s_len, SSD_GROUPS, heads)
    dtc = jnp.transpose(jnp.repeat(dt, rep, axis=-1), (0, 2, 1, 3))
    dtr = jnp.swapaxes(dtc, 2, 3)
    par = jnp.repeat(jnp.stack([dt_bias, a_log]).reshape(2, SSD_GROUPS, heads), rep, axis=-1)
    pc = jnp.transpose(par, (1, 0, 2))
    pr = jnp.transpose(par, (1, 2, 0))
    dsk = jnp.repeat(d_skip, SSD_HEAD_DIM).reshape(SSD_GROUPS, 1, gw)
    b0 = d_ssd // SSD_STATE
    c0 = b0 + SSD_GROUPS
    z0 = z_off // gw
    return pl.pallas_call(
        functools.partial(_ssd_kernel, heads=heads),
        grid=(bsz, SSD_GROUPS, nc),
        in_specs=[pl.BlockSpec((SSD_CHUNK, gw), lambda b, g, c: (b * nc + c, g)),
                  pl.BlockSpec((SSD_CHUNK, SSD_STATE), lambda b, g, c: (b * nc + c, b0 + g)),
                  pl.BlockSpec((SSD_CHUNK, SSD_STATE), lambda b, g, c: (b * nc + c, c0 + g)),
                  pl.BlockSpec((SSD_CHUNK, gw), lambda b, g, c: (b * nc + c, z0 + g)),
                  pl.BlockSpec((1, 1, SSD_CHUNK, LANES), lambda b, g, c: (b, g, c, 0)),
                  pl.BlockSpec((1, 1, LANES, SSD_CHUNK), lambda b, g, c: (b, g, 0, c)),
                  pl.BlockSpec((1, 2, LANES), lambda b, g, c: (g, 0, 0)),
                  pl.BlockSpec((1, LANES, 2), lambda b, g, c: (g, 0, 0)),
                  pl.BlockSpec((1, 1, gw), lambda b, g, c: (g, 0, 0)),
                  pl.BlockSpec((1, gw), lambda b, g, c: (0, g))],
        out_specs=pl.BlockSpec((SSD_CHUNK, gw), lambda b, g, c: (b * nc + c, g)),
        out_shape=jax.ShapeDtypeStruct((m, d_ssd), BF16),
        scratch_shapes=[pltpu.VMEM((SSD_STATE, gw), F32)],
        compiler_params=_cparams(("parallel", "parallel", "arbitrary")),
        name="ssd",
    )(conv_out, conv_out, conv_out, proj, dtc, dtr, pc, pr, dsk, norm_g.reshape(1, d_ssd))


def _ffn(hn, x, w13, w2, g_post, g_next):
    h = _swiglu_up(hn, w13.astype(BF16))
    mm = _matmul(h, w2.astype(BF16), F32, name="ffn_down")
    return _post(mm, x, g_post, g_next, 0.5)


def _mixer(hn, x, pos, w_in, idx_ln_g, idx_ln_b, conv_w, conv_b, dt_bias, a_log, d_skip, ssd_norm_g,
           w_attn_o, w_ssd_o, w_out, g_post, g_next, bsz, s_len):
    d = x.shape[1]
    d_ssd = ssd_norm_g.shape[0]
    n_ssd_heads = a_log.shape[0]
    gn = SSD_GROUPS * SSD_STATE
    sizes = (D_ATTN, D_KV, D_KV, D_IDX, IDX_DIM, IDX_HEADS, d_ssd, d_ssd, gn, gn, n_ssd_heads, d, d)
    names = ("q", "k", "v", "qi", "ki", "wi", "z", "xs", "bs", "cs", "dt", "ga", "gs")
    starts = np.concatenate([[0], np.cumsum(sizes)[:-1]])
    col = {nm: w_in[:, int(s0):int(s0) + sz] for nm, s0, sz in zip(names, starts, sizes)}
    order = ("q", "qi", "k", "v", "z", "xs", "bs", "cs", "ga", "gs")
    w_big = jnp.concatenate([col[nm] for nm in order], axis=1).astype(BF16)
    offs, o = {}, 0
    for nm in order:
        offs[nm] = o
        o += col[nm].shape[1]
    used = 2 * IDX_DIM + n_ssd_heads + IDX_HEADS
    n_small = -(-used // LANES) * LANES
    w_small = jnp.concatenate([col["ki"], col["ki"], col["dt"], col["wi"],
                               jnp.zeros((d, n_small - used), w_in.dtype)], axis=1).astype(BF16)
    dt_off = 2 * IDX_DIM
    wi_off = dt_off + n_ssd_heads

    proj = _matmul(hn, w_big, BF16, name="in_proj")
    small = _matmul(hn, w_small, F32, name="in_proj_small")

    qs, qis, kr, kid = _attn_prep(pos, proj, small, offs, idx_ln_g, idx_ln_b)
    o_attn = _dsa(qs, qis, small, kid, kr, proj, offs, bsz, s_len, wi_off)

    conv_out = _conv(proj, offs["xs"], conv_w, conv_b, s_len)
    y = _ssd(conv_out, proj, small, offs["z"], dt_off, dt_bias, a_log, d_skip, ssd_norm_g, bsz, s_len)

    merged = _merge(o_attn, w_attn_o.astype(BF16), y, w_ssd_o.astype(BF16), proj, offs["ga"], offs["gs"], d)
    mm = _matmul(merged, w_out.astype(BF16), F32, name="out_proj")
    return _post(mm, x, g_post, g_next, 1.0)


def kernel(x, positions, ffn1_pre_g, ffn1_w13, ffn1_w2, ffn1_post_g, mix_pre_g, w_in, idx_ln_g, idx_ln_b, conv_w, conv_b, dt_bias, a_log, d_skip, ssd_norm_g, w_attn_o, w_ssd_o, w_out, mix_post_g, ffn2_pre_g, ffn2_w13, ffn2_w2, ffn2_post_g):
    bsz, s_len, d = x.shape
    depth = ffn1_w13.shape[0]
    assert s_len % KEY_BLOCK == 0 and s_len % SSD_CHUNK == 0 and d % LANES == 0
    m = bsz * s_len
    xf = x.reshape(m, d)
    pos = positions.reshape(m, 1)
    hn = _prenorm(xf, ffn1_pre_g[0])
    for l in range(depth):
        xf, hn = _ffn(hn, xf, ffn1_w13[l], ffn1_w2[l], ffn1_post_g[l], mix_pre_g[l])
        xf, hn = _mixer(hn, xf, pos, w_in[l], idx_ln_g[l], idx_ln_b[l], conv_w[l], conv_b[l], dt_bias[l],
                        a_log[l], d_skip[l], ssd_norm_g[l], w_attn_o[l], w_ssd_o[l], w_out[l],
                        mix_post_g[l], ffn2_pre_g[l], bsz, s_len)
        g_next = ffn1_pre_g[l + 1] if l + 1 < depth else None
        xf, hn = _ffn(hn, xf, ffn2_w13[l], ffn2_w2[l], ffn2_post_g[l], g_next)
    return xf.reshape(bsz, s_len, d)
```

```python
import functools

import jax
import jax.numpy as jnp
import numpy as np
from jax import lax
from jax.experimental import pallas as pl
from jax.experimental.pallas import tpu as pltpu

N_HEADS = 16
N_KV_HEADS = 4
HEAD_DIM = 128
GQA = N_HEADS // N_KV_HEADS
D_ATTN = N_HEADS * HEAD_DIM
D_KV = N_KV_HEADS * HEAD_DIM
IDX_HEADS = 32
IDX_DIM = 64
D_IDX = IDX_HEADS * IDX_DIM
MAX_TOPK = 256
Q_BLOCK = 128
ROPE_THETA = 10000.0
SSD_HEAD_DIM = 64
SSD_GROUPS = 8
SSD_STATE = 128
CONV_WIDTH = 4
SSD_CHUNK = 128
EPS = 1e-6

LANES = 128
SUBLANES = 8
VMEM_LIMIT = 56 * 1024 * 1024

KEY_BLOCK = 256
IDX_HEAD_GROUP = 4
NEG_BIG = -1e30
INT_MIN = -(2 ** 31)

BF16 = jnp.bfloat16
F32 = jnp.float32


def _cparams(sem):
    return pltpu.CompilerParams(dimension_semantics=sem, vmem_limit_bytes=VMEM_LIMIT)


def _pick(n, pref):
    t = min(pref, n)
    while n % t:
        t //= 2
    return t


def _rms(v, g):
    return v * lax.rsqrt(jnp.mean(v * v, axis=-1, keepdims=True) + EPS) * g


def _prenorm_kernel(x_ref, g_ref, o_ref):
    o_ref[...] = _rms(x_ref[...], g_ref[...]).astype(o_ref.dtype)


def _prenorm(x, g):
    m, d = x.shape
    tr = _pick(m, 256)
    return pl.pallas_call(
        _prenorm_kernel,
        grid=(m // tr,),
        in_specs=[pl.BlockSpec((tr, d), lambda i: (i, 0)), pl.BlockSpec((1, d), lambda i: (0, 0))],
        out_specs=pl.BlockSpec((tr, d), lambda i: (i, 0)),
        out_shape=jax.ShapeDtypeStruct((m, d), BF16),
        compiler_params=_cparams(("parallel",)),
        name="prenorm",
    )(x, g.reshape(1, d))


def _post_kernel(m_ref, x_ref, gp_ref, gn_ref, xo_ref, hn_ref, *, coef):
    xn = x_ref[...] + coef * _rms(m_ref[...], gp_ref[...])
    xo_ref[...] = xn
    hn_ref[...] = _rms(xn, gn_ref[...]).astype(hn_ref.dtype)


def _post_last_kernel(m_ref, x_ref, gp_ref, xo_ref, *, coef):
    xo_ref[...] = x_ref[...] + coef * _rms(m_ref[...], gp_ref[...])


def _post(mm, x, g_post, g_next, coef):
    m, d = x.shape
    tr = _pick(m, 256)
    row = pl.BlockSpec((tr, d), lambda i: (i, 0))
    vec = pl.BlockSpec((1, d), lambda i: (0, 0))
    if g_next is None:
        return pl.pallas_call(
            functools.partial(_post_last_kernel, coef=coef),
            grid=(m // tr,),
            in_specs=[row, row, vec],
            out_specs=row,
            out_shape=jax.ShapeDtypeStruct((m, d), F32),
            compiler_params=_cparams(("parallel",)),
            name="post_last",
        )(mm, x, g_post.reshape(1, d)), None
    return pl.pallas_call(
        functools.partial(_post_kernel, coef=coef),
        grid=(m // tr,),
        in_specs=[row, row, vec, vec],
        out_specs=[row, row],
        out_shape=[jax.ShapeDtypeStruct((m, d), F32), jax.ShapeDtypeStruct((m, d), BF16)],
        compiler_params=_cparams(("parallel",)),
        name="post",
    )(mm, x, g_post.reshape(1, d), g_next.reshape(1, d))


def _mm_kernel(x_ref, w_ref, o_ref, *scratch, nk):
    part = jnp.dot(x_ref[...], w_ref[...], preferred_element_type=F32)
    if nk == 1:
        o_ref[...] = part.astype(o_ref.dtype)
        return
    acc_ref, = scratch
    k = pl.program_id(2)

    @pl.when(k == 0)
    def _():
        acc_ref[...] = part

    @pl.when(k > 0)
    def _():
        acc_ref[...] += part

    @pl.when(k == nk - 1)
    def _():
        o_ref[...] = acc_ref[...].astype(o_ref.dtype)


def _matmul(x, w, out_dtype, tm=1024, tn=1024, tk=4096, name="matmul"):
    m, kd = x.shape
    n = w.shape[1]
    tm, tn, tk = _pick(m, tm), _pick(n, tn), _pick(kd, tk)
    nk = kd // tk
    return pl.pallas_call(
        functools.partial(_mm_kernel, nk=nk),
        grid=(m // tm, n // tn, nk),
        in_specs=[pl.BlockSpec((tm, tk), lambda i, j, k: (i, k)),
                  pl.BlockSpec((tk, tn), lambda i, j, k: (k, j))],
        out_specs=pl.BlockSpec((tm, tn), lambda i, j, k: (i, j)),
        out_shape=jax.ShapeDtypeStruct((m, n), out_dtype),
        scratch_shapes=[pltpu.VMEM((tm, tn), F32)] if nk > 1 else [],
        compiler_params=_cparams(("parallel", "parallel", "arbitrary")),
        name=name,
    )(x, w)


def _swiglu_kernel(x_ref, w1_ref, w3_ref, o_ref):
    x = x_ref[...]
    a = jnp.dot(x, w1_ref[...], preferred_element_type=F32)
    b = jnp.dot(x, w3_ref[...], preferred_element_type=F32)
    o_ref[...] = (a * jax.nn.sigmoid(a) * b).astype(o_ref.dtype)


def _swiglu_up(hn, w13):
    m, kd = hn.shape
    f = w13.shape[1] // 2
    tm, tn = _pick(m, 1024), _pick(f, 512)
    nj = f // tn
    return pl.pallas_call(
        _swiglu_kernel,
        grid=(m // tm, nj),
        in_specs=[pl.BlockSpec((tm, kd), lambda i, j: (i, 0)),
                  pl.BlockSpec((kd, tn), lambda i, j: (0, j)),
                  pl.BlockSpec((kd, tn), lambda i, j: (0, j + nj))],
        out_specs=pl.BlockSpec((tm, tn), lambda i, j: (i, j)),
        out_shape=jax.ShapeDtypeStruct((m, f), BF16),
        compiler_params=_cparams(("parallel", "parallel")),
        name="swiglu_up",
    )(hn, w13, w13)


def _merge_kernel(o_ref, wa_ref, y_ref, ws_ref, ga_ref, gs_ref, out_ref):
    a = jnp.dot(o_ref[...], wa_ref[...], preferred_element_type=F32)
    s = jnp.dot(y_ref[...], ws_ref[...], preferred_element_type=F32)
    ga = jax.nn.sigmoid(ga_ref[...].astype(F32))
    gs = jax.nn.sigmoid(gs_ref[...].astype(F32))
    out_ref[...] = (ga * a + gs * s).astype(out_ref.dtype)


def _merge(o_attn, wa, y, ws, proj, ga_off, gs_off, d):
    m = o_attn.shape[0]
    tm, tn = _pick(m, 1024), _pick(d, 512)
    ja, js = ga_off // tn, gs_off // tn
    return pl.pallas_call(
        _merge_kernel,
        grid=(m // tm, d // tn),
        in_specs=[pl.BlockSpec((tm, o_attn.shape[1]), lambda i, j: (i, 0)),
                  pl.BlockSpec((wa.shape[0], tn), lambda i, j: (0, j)),
                  pl.BlockSpec((tm, y.shape[1]), lambda i, j: (i, 0)),
                  pl.BlockSpec((ws.shape[0], tn), lambda i, j: (0, j)),
                  pl.BlockSpec((tm, tn), lambda i, j: (i, ja + j)),
                  pl.BlockSpec((tm, tn), lambda i, j: (i, js + j))],
        out_specs=pl.BlockSpec((tm, tn), lambda i, j: (i, j)),
        out_shape=jax.ShapeDtypeStruct((m, d), BF16),
        compiler_params=_cparams(("parallel", "parallel")),
        name="merge",
    )(o_attn, wa, y, ws, proj, proj)


def _rope_tables(pos, invf, sign):
    ang = pos * invf
    return jnp.cos(ang), jnp.sin(ang) * sign


def _rope128(x, cos, sin_signed):
    return x * cos + pltpu.roll(x, HEAD_DIM // 2, 1) * sin_signed


def _rope64(x, cos, sin_signed, first_half):
    half = IDX_DIM // 2
    rot = jnp.where(first_half, pltpu.roll(x, LANES - half, 1), pltpu.roll(x, half, 1))
    return x * cos + rot * sin_signed


def _attn_prep_kernel(pos_ref, q_ref, qi_ref, k_ref, sm_ref, f128_ref, s128_ref, f64_ref, s64_ref,
                      lng_ref, lnb_ref, qs_ref, qis_ref, kr_ref, kid_ref):
    pos = pos_ref[...].astype(F32)
    cos_a, sin_a = _rope_tables(pos, f128_ref[...], s128_ref[...])
    cos_b, sin_b = _rope_tables(pos, f64_ref[...], s64_ref[...])
    lane = lax.broadcasted_iota(jnp.int32, (Q_BLOCK, LANES), 1)
    first_half = (lane & (IDX_DIM - 1)) < IDX_DIM // 2
    low_head = lane < IDX_DIM
    scale = HEAD_DIM ** -0.5
    for h in range(N_HEADS):
        xh = q_ref[:, h * HEAD_DIM:(h + 1) * HEAD_DIM].astype(F32)
        r = _rope128(xh, cos_a, sin_a) * scale
        hk, g = divmod(h, GQA)
        qs_ref[0, hk, g * Q_BLOCK:(g + 1) * Q_BLOCK, :] = r.astype(qs_ref.dtype)
    for h in range(N_KV_HEADS):
        xh = k_ref[:, h * HEAD_DIM:(h + 1) * HEAD_DIM].astype(F32)
        kr_ref[:, h * HEAD_DIM:(h + 1) * HEAD_DIM] = _rope128(xh, cos_a, sin_a).astype(kr_ref.dtype)
    zero = jnp.zeros((Q_BLOCK, LANES), F32)
    for j in range(IDX_HEADS // 2):
        xp = qi_ref[:, j * LANES:(j + 1) * LANES].astype(F32)
        r = _rope64(xp, cos_b, sin_b, first_half)
        qis_ref[0, 2 * j] = jnp.where(low_head, r, zero).astype(qis_ref.dtype)
        qis_ref[0, 2 * j + 1] = jnp.where(low_head, zero, r).astype(qis_ref.dtype)
    ki = sm_ref[:, 0:LANES]
    mu = jnp.mean(ki, axis=-1, keepdims=True)
    var = jnp.mean(jnp.square(ki - mu), axis=-1, keepdims=True)
    kn = (ki - mu) * lax.rsqrt(var + EPS) * lng_ref[...] + lnb_ref[...]
    kid_ref[...] = _rope64(kn, cos_b, sin_b, first_half).astype(kid_ref.dtype)


def _attn_prep(pos, proj, small, offs, idx_ln_g, idx_ln_b):
    m = proj.shape[0]
    nblk = m // Q_BLOCK
    half_a, half_b = HEAD_DIM // 2, IDX_DIM // 2
    inv_a = 1.0 / (ROPE_THETA ** (jnp.arange(half_a, dtype=F32) * (2.0 / HEAD_DIM)))
    inv_b = 1.0 / (ROPE_THETA ** (jnp.arange(half_b, dtype=F32) * (2.0 / IDX_DIM)))
    f128 = jnp.tile(inv_a, 2).reshape(1, LANES)
    f64 = jnp.tile(inv_b, 4).reshape(1, LANES)
    s128 = jnp.asarray(np.repeat([-1.0, 1.0], half_a), F32).reshape(1, LANES)
    s64 = jnp.asarray(np.tile(np.repeat([-1.0, 1.0], half_b), 2), F32).reshape(1, LANES)
    lng = jnp.tile(idx_ln_g, 2).reshape(1, LANES)
    lnb = jnp.tile(idx_ln_b, 2).reshape(1, LANES)
    vec = pl.BlockSpec((1, LANES), lambda i: (0, 0))
    return pl.pallas_call(
        _attn_prep_kernel,
        grid=(nblk,),
        in_specs=[pl.BlockSpec((Q_BLOCK, 1), lambda i: (i, 0)),
                  pl.BlockSpec((Q_BLOCK, D_ATTN), lambda i: (i, offs["q"] // D_ATTN)),
                  pl.BlockSpec((Q_BLOCK, D_IDX), lambda i: (i, offs["qi"] // D_IDX)),
                  pl.BlockSpec((Q_BLOCK, D_KV), lambda i: (i, offs["k"] // D_KV)),
                  pl.BlockSpec((Q_BLOCK, small.shape[1]), lambda i: (i, 0)),
                  vec, vec, vec, vec, vec, vec],
        out_specs=[pl.BlockSpec((1, N_KV_HEADS, GQA * Q_BLOCK, HEAD_DIM), lambda i: (i, 0, 0, 0)),
                   pl.BlockSpec((1, IDX_HEADS, Q_BLOCK, LANES), lambda i: (i, 0, 0, 0)),
                   pl.BlockSpec((Q_BLOCK, D_KV), lambda i: (i, 0)),
                   pl.BlockSpec((Q_BLOCK, LANES), lambda i: (i, 0))],
        out_shape=[jax.ShapeDtypeStruct((nblk, N_KV_HEADS, GQA * Q_BLOCK, HEAD_DIM), BF16),
                   jax.ShapeDtypeStruct((nblk, IDX_HEADS, Q_BLOCK, LANES), BF16),
                   jax.ShapeDtypeStruct((m, D_KV), BF16),
                   jax.ShapeDtypeStruct((m, LANES), BF16)],
        compiler_params=_cparams(("parallel",)),
        name="attn_prep",
    )(pos, proj, proj, proj, small, f128, s128, f64, s64, lng, lnb)


def _dsa_kernel(qs_ref, qis_ref, sm_ref, kid_ref, kr_ref, v_ref, o_ref,
                wb_scr, key_scr, bias_scr, m_scr, l_scr, acc_scr, *, topk, wi_off):
    i = pl.program_id(1)
    nkb = (i * Q_BLOCK + Q_BLOCK + KEY_BLOCK - 1) // KEY_BLOCK
    nsub = KEY_BLOCK // LANES
    wi_scale = IDX_HEADS ** -0.5 * IDX_DIM ** -0.5

    for h in range(IDX_HEADS):
        col = sm_ref[:, wi_off + h:wi_off + h + 1] * wi_scale
        wb_scr[h] = jnp.broadcast_to(col, (Q_BLOCK, LANES))

    row_pos = i * Q_BLOCK + lax.broadcasted_iota(jnp.int32, (Q_BLOCK, KEY_BLOCK), 0)
    col_iota = lax.broadcasted_iota(jnp.int32, (Q_BLOCK, KEY_BLOCK), 1)

    def score_block(kb, carry):
        k0 = pl.multiple_of(kb * KEY_BLOCK, KEY_BLOCK)
        ki = kid_ref[pl.ds(k0, KEY_BLOCK), :]
        score = jnp.zeros((Q_BLOCK, KEY_BLOCK), F32)
        for hg in range(IDX_HEADS // IDX_HEAD_GROUP):
            h0 = hg * IDX_HEAD_GROUP
            q2 = qis_ref[0, h0:h0 + IDX_HEAD_GROUP].reshape(IDX_HEAD_GROUP * Q_BLOCK, LANES)
            d = lax.dot_general(q2, ki, (((1,), (1,)), ((), ())), preferred_element_type=F32)
            for u in range(IDX_HEAD_GROUP):
                w = wb_scr[h0 + u]
                score = score + jnp.maximum(d[u * Q_BLOCK:(u + 1) * Q_BLOCK], 0.0) * jnp.concatenate([w] * nsub, axis=1)
        bits = lax.bitcast_convert_type(score, jnp.int32)
        skey = bits ^ ((bits >> 31) & jnp.int32(0x7FFFFFFF))
        causal = (k0 + col_iota) <= row_pos
        key_scr[kb] = jnp.where(causal, skey, jnp.int32(INT_MIN))
        return carry

    lax.fori_loop(0, nkb, score_block, 0)

    def bit_step(it, thr):
        cand = thr ^ lax.shift_left(jnp.int32(1), 31 - it)
        cand_b = jnp.broadcast_to(cand, (Q_BLOCK, LANES))

        def count_block(kb, acc):
            blk = key_scr[kb]
            for c in range(nsub):
                acc = acc + jnp.where(blk[:, c * LANES:(c + 1) * LANES] >= cand_b, 1, 0)
            return acc

        cnt = lax.fori_loop(0, nkb, count_block, jnp.zeros((Q_BLOCK, LANES), jnp.int32))
        total = jnp.sum(cnt, axis=1, keepdims=True)
        return jnp.where(total >= topk, cand, thr)

    thr = lax.fori_loop(0, 32, bit_step, jnp.full((Q_BLOCK, 1), INT_MIN, jnp.int32))
    thr = jnp.maximum(thr, jnp.int32(INT_MIN + 1))
    thr_b = jnp.broadcast_to(thr, (Q_BLOCK, KEY_BLOCK))

    def bias_block(kb, carry):
        bias_scr[kb] = jnp.where(key_scr[kb] >= thr_b, 0.0, NEG_BIG).astype(F32)
        return carry

    lax.fori_loop(0, nkb, bias_block, 0)

    def masked_logits(kb, hk):
        k0 = pl.multiple_of(kb * KEY_BLOCK, KEY_BLOCK)
        kblk = kr_ref[pl.ds(k0, KEY_BLOCK), hk * HEAD_DIM:(hk + 1) * HEAD_DIM]
        s = lax.dot_general(qs_ref[0, hk], kblk, (((1,), (1,)), ((), ())), preferred_element_type=F32)
        return s + jnp.concatenate([bias_scr[kb]] * GQA, axis=0)

    m_scr[...] = jnp.full(m_scr.shape, NEG_BIG, F32)

    def max_block(kb, carry):
        for hk in range(N_KV_HEADS):
            s = masked_logits(kb, hk)
            part = s[:, 0:LANES]
            for c in range(1, nsub):
                part = jnp.maximum(part, s[:, c * LANES:(c + 1) * LANES])
            m_scr[hk] = jnp.maximum(m_scr[hk], part)
        return carry

    lax.fori_loop(0, nkb, max_block, 0)
    for hk in range(N_KV_HEADS):
        row_max = jnp.max(m_scr[hk], axis=1, keepdims=True)
        m_scr[hk] = jnp.broadcast_to(row_max, m_scr.shape[1:])
    l_scr[...] = jnp.zeros(l_scr.shape, F32)
    acc_scr[...] = jnp.zeros(acc_scr.shape, F32)

    def pv_block(kb, carry):
        k0 = pl.multiple_of(kb * KEY_BLOCK, KEY_BLOCK)
        for hk in range(N_KV_HEADS):
            vblk = v_ref[pl.ds(k0, KEY_BLOCK), hk * HEAD_DIM:(hk + 1) * HEAD_DIM]
            p = jnp.exp(masked_logits(kb, hk) - jnp.concatenate([m_scr[hk]] * nsub, axis=1))
            part = p[:, 0:LANES]
            for c in range(1, nsub):
                part = part + p[:, c * LANES:(c + 1) * LANES]
            l_scr[hk] += part
            acc_scr[hk] += jnp.dot(p.astype(BF16), vblk, preferred_element_type=F32)
        return carry

    lax.fori_loop(0, nkb, pv_block, 0)
    for hk in range(N_KV_HEADS):
        out = acc_scr[hk] / jnp.sum(l_scr[hk], axis=1, keepdims=True)
        for g in range(GQA):
            h = hk * GQA + g
            o_ref[:, h * HEAD_DIM:(h + 1) * HEAD_DIM] = out[g * Q_BLOCK:(g + 1) * Q_BLOCK].astype(o_ref.dtype)


def _dsa(qs, qis, small, kid, kr, proj, offs, bsz, s_len, wi_off):
    nblk = s_len // Q_BLOCK
    nkb = s_len // KEY_BLOCK
    topk = min(MAX_TOPK, s_len // 4)
    vcol = offs["v"] // D_KV
    return pl.pallas_call(
        functools.partial(_dsa_kernel, topk=topk, wi_off=wi_off),
        grid=(bsz, nblk),
        in_specs=[pl.BlockSpec((1, N_KV_HEADS, GQA * Q_BLOCK, HEAD_DIM), lambda b, i: (b * nblk + i, 0, 0, 0)),
                  pl.BlockSpec((1, IDX_HEADS, Q_BLOCK, LANES), lambda b, i: (b * nblk + i, 0, 0, 0)),
                  pl.BlockSpec((Q_BLOCK, small.shape[1]), lambda b, i: (b * nblk + i, 0)),
                  pl.BlockSpec((s_len, LANES), lambda b, i: (b, 0)),
                  pl.BlockSpec((s_len, D_KV), lambda b, i: (b, 0)),
                  pl.BlockSpec((s_len, D_KV), lambda b, i: (b, vcol))],
        out_specs=pl.BlockSpec((Q_BLOCK, D_ATTN), lambda b, i: (b * nblk + i, 0)),
        out_shape=jax.ShapeDtypeStruct((bsz * s_len, D_ATTN), BF16),
        scratch_shapes=[pltpu.VMEM((IDX_HEADS, Q_BLOCK, LANES), F32),
                        pltpu.VMEM((nkb, Q_BLOCK, KEY_BLOCK), jnp.int32),
                        pltpu.VMEM((nkb, Q_BLOCK, KEY_BLOCK), F32),
                        pltpu.VMEM((N_KV_HEADS, GQA * Q_BLOCK, LANES), F32),
                        pltpu.VMEM((N_KV_HEADS, GQA * Q_BLOCK, LANES), F32),
                        pltpu.VMEM((N_KV_HEADS, GQA * Q_BLOCK, HEAD_DIM), F32)],
        compiler_params=_cparams(("parallel", "arbitrary")),
        name="dsa",
    )(qs, qis, small, kid, kr, proj)


def _conv_kernel(x_ref, halo_ref, w_ref, b_ref, o_ref, *, tiles_per_seq):
    st = x_ref.shape[0]
    first = (pl.program_id(0) % tiles_per_seq) == 0
    halo = jnp.where(first, 0.0, halo_ref[...].astype(F32))
    xx = jnp.concatenate([halo, x_ref[...].astype(F32)], axis=0)
    acc = b_ref[...] + w_ref[CONV_WIDTH - 1:CONV_WIDTH, :] * xx[SUBLANES:, :]
    for k in range(CONV_WIDTH - 1):
        shifted = pltpu.roll(xx, CONV_WIDTH - 1 - k, 0)[SUBLANES:, :]
        acc = acc + w_ref[k:k + 1, :] * shifted
    o_ref[...] = (acc * jax.nn.sigmoid(acc)).astype(o_ref.dtype)


def _conv(proj, xs_off, conv_w, conv_b, s_len):
    m = proj.shape[0]
    ch = conv_w.shape[1]
    st, ct = _pick(s_len, 512), _pick(ch, 512)
    c0 = xs_off // ct
    hb = st // SUBLANES
    return pl.pallas_call(
        functools.partial(_conv_kernel, tiles_per_seq=s_len // st),
        grid=(m // st, ch // ct),
        in_specs=[pl.BlockSpec((st, ct), lambda r, c: (r, c0 + c)),
                  pl.BlockSpec((SUBLANES, ct), lambda r, c: (jnp.maximum(r * hb - 1, 0), c0 + c)),
                  pl.BlockSpec((CONV_WIDTH, ct), lambda r, c: (0, c)),
                  pl.BlockSpec((1, ct), lambda r, c: (0, c))],
        out_specs=pl.BlockSpec((st, ct), lambda r, c: (r, c)),
        out_shape=jax.ShapeDtypeStruct((m, ch), BF16),
        compiler_params=_cparams(("parallel", "parallel")),
        name="conv",
    )(proj, proj, conv_w, conv_b.reshape(1, ch))


def _softplus(x):
    return jnp.maximum(x, 0.0) + jnp.log1p(jnp.exp(-jnp.abs(x)))


def _ssd_kernel(x_ref, b_ref, c_ref, z_ref, dtc_ref, dtr_ref, pc_ref, pr_ref, dsk_ref, ng_ref, o_ref,
                state_scr, *, heads):
    gw = heads * SSD_HEAD_DIM
    rep = LANES // heads
    hi = lax.Precision.HIGHEST
    cl = SSD_CHUNK

    @pl.when(pl.program_id(2) == 0)
    def _():
        state_scr[...] = jnp.zeros(state_scr.shape, F32)

    dtc = _softplus(dtc_ref[0, 0] + pc_ref[0, 0:1, :])
    dtr = _softplus(dtr_ref[0, 0] + pr_ref[0, :, 0:1])
    adt_c = dtc * -jnp.exp(pc_ref[0, 1:2, :])
    adt_r = dtr * -jnp.exp(pr_ref[0, :, 1:2])
    li = lax.broadcasted_iota(jnp.int32, (cl, cl), 0)
    si = lax.broadcasted_iota(jnp.int32, (cl, cl), 1)
    tri = li >= si
    tri_f = tri.astype(F32)
    acs_c = jnp.dot(tri_f, adt_c, preferred_element_type=F32, precision=hi)
    acs_r = lax.dot_general(adt_r, tri_f, (((1,), (1,)), ((), ())), preferred_element_type=F32, precision=hi)
    last = acs_c[cl - 1:cl, :]
    er = lax.broadcasted_iota(jnp.int32, (LANES, gw), 0)
    ec = lax.broadcasted_iota(jnp.int32, (LANES, gw), 1)
    expand = (er == lax.shift_right_logical(ec, SSD_HEAD_DIM.bit_length() - 1) * rep).astype(BF16)
    stacked = jnp.concatenate([dtc, dtc * jnp.exp(last - acs_c), jnp.exp(acs_c)], axis=0)
    st_hi = stacked.astype(BF16)
    st_lo = (stacked - st_hi.astype(F32)).astype(BF16)
    ex = (jnp.dot(st_hi, expand, preferred_element_type=F32) + jnp.dot(st_lo, expand, preferred_element_type=F32))
    dt_e, dts_e, dec_e = ex[0:cl], ex[cl:2 * cl], ex[2 * cl:3 * cl]

    xg = x_ref[...].astype(F32)
    xdt = (xg * dt_e).astype(BF16)
    xst = (xg * dts_e).astype(BF16)
    bm = b_ref[...]
    cm = c_ref[...]
    cb = lax.dot_general(cm, bm, (((1,), (1,)), ((), ())), preferred_element_type=F32)
    lane = lax.broadcasted_iota(jnp.int32, (cl, LANES), 1)
    pieces = []
    for j in range(gw // LANES):
        ys = []
        for r in (2 * j, 2 * j + 1):
            col = jnp.broadcast_to(acs_c[:, r * rep:r * rep + 1], (cl, cl))
            row = jnp.broadcast_to(acs_r[r * rep:r * rep + 1, :], (cl, cl))
            lmat = jnp.where(tri, jnp.exp(col - row), 0.0)
            ys.append(jnp.dot((cb * lmat).astype(BF16), xdt[:, j * LANES:(j + 1) * LANES],
                              preferred_element_type=F32))
        pieces.append(jnp.where(lane < SSD_HEAD_DIM, ys[0], ys[1]))
    y_diag = jnp.concatenate(pieces, axis=1) if len(pieces) > 1 else pieces[0]

    state = state_scr[...]
    y_off = jnp.dot(cm, state.astype(BF16), preferred_element_type=F32) * dec_e
    y = y_diag + y_off + xg * dsk_ref[0]
    state_scr[...] = state * dec_e[cl - 1:cl, :] + lax.dot_general(
        bm, xst, (((0,), (0,)), ((), ())), preferred_element_type=F32)

    z = z_ref[...].astype(F32)
    yz = y * (z * jax.nn.sigmoid(z))
    o_ref[...] = _rms(yz, ng_ref[...]).astype(o_ref.dtype)


def _ssd(conv_out, proj, small, z_off, dt_off, dt_bias, a_log, d_skip, norm_g, bsz, s_len):
    d_ssd = norm_g.shape[0]
    n_heads = a_log.shape[0]
    heads = n_heads // SSD_GROUPS
    gw = heads * SSD_HEAD_DIM
    rep = LANES // heads
    nc = s_len // SSD_CHUNK
    m = bsz * s_len
    dt = small[:, dt_off:dt_off + n_heads].reshape(bsz, s_len, SSD_GROUPS, heads)
    dtc = jnp.transpose(jnp.repeat(dt, rep, axis=-1), (0, 2, 1, 3))
    dtr = jnp.swapaxes(dtc, 2, 3)
    par = jnp.repeat(jnp.stack([dt_bias, a_log]).reshape(2, SSD_GROUPS, heads), rep, axis=-1)
    pc = jnp.transpose(par, (1, 0, 2))
    pr = jnp.transpose(par, (1, 2, 0))
    dsk = jnp.repeat(d_skip, SSD_HEAD_DIM).reshape(SSD_GROUPS, 1, gw)
    b0 = d_ssd // SSD_STATE
    c0 = b0 + SSD_GROUPS
    z0 = z_off // gw
    return pl.pallas_call(
        functools.partial(_ssd_kernel, heads=heads),
        grid=(bsz, SSD_GROUPS, nc),
        in_specs=[pl.BlockSpec((SSD_CHUNK, gw), lambda b, g, c: (b * nc + c, g)),
                  pl.BlockSpec((SSD_CHUNK, SSD_STATE), lambda b, g, c: (b * nc + c, b0 + g)),
                  pl.BlockSpec((SSD_CHUNK, SSD_STATE), lambda b, g, c: (b * nc + c, c0 + g)),
                  pl.BlockSpec((SSD_CHUNK, gw), lambda b, g, c: (b * nc + c, z0 + g)),
                  pl.BlockSpec((1, 1, SSD_CHUNK, LANES), lambda b, g, c: (b, g, c, 0)),
                  pl.BlockSpec((1, 1, LANES, SSD_CHUNK), lambda b, g, c: (b, g, 0, c)),
                  pl.BlockSpec((1, 2, LANES), lambda b, g, c: (g, 0, 0)),
                  pl.BlockSpec((1, LANES, 2), lambda b, g, c: (g, 0, 0)),
                  pl.BlockSpec((1, 1, gw), lambda b, g, c: (g, 0, 0)),
                  pl.BlockSpec((1, gw), lambda b, g, c: (0, g))],
        out_specs=pl.BlockSpec((SSD_CHUNK, gw), lambda b, g, c: (b * nc + c, g)),
        out_shape=jax.ShapeDtypeStruct((m, d_ssd), BF16),
        scratch_shapes=[pltpu.VMEM((SSD_STATE, gw), F32)],
        compiler_params=_cparams(("parallel", "parallel", "arbitrary")),
        name="ssd",
    )(conv_out, conv_out, conv_out, proj, dtc, dtr, pc, pr, dsk, norm_g.reshape(1, d_ssd))


def _ffn(hn, x, w13, w2, g_post, g_next):
    h = _swiglu_up(hn, w13.astype(BF16))
    mm = _matmul(h, w2.astype(BF16), F32, name="ffn_down")
    return _post(mm, x, g_post, g_next, 0.5)


def _mixer(hn, x, pos, w_in, idx_ln_g, idx_ln_b, conv_w, conv_b, dt_bias, a_log, d_skip, ssd_norm_g,
           w_attn_o, w_ssd_o, w_out, g_post, g_next, bsz, s_len):
    d = x.shape[1]
    d_ssd = ssd_norm_g.shape[0]
    n_ssd_heads = a_log.shape[0]
    gn = SSD_GROUPS * SSD_STATE
    sizes = (D_ATTN, D_KV, D_KV, D_IDX, IDX_DIM, IDX_HEADS, d_ssd, d_ssd, gn, gn, n_ssd_heads, d, d)
    names = ("q", "k", "v", "qi", "ki", "wi", "z", "xs", "bs", "cs", "dt", "ga", "gs")
    starts = np.concatenate([[0], np.cumsum(sizes)[:-1]])
    col = {nm: w_in[:, int(s0):int(s0) + sz] for nm, s0, sz in zip(names, starts, sizes)}
    order = ("q", "qi", "k", "v", "z", "xs", "bs", "cs", "ga", "gs")
    w_big = jnp.concatenate([col[nm] for nm in order], axis=1).astype(BF16)
    offs, o = {}, 0
    for nm in order:
        offs[nm] = o
        o += col[nm].shape[1]
    used = 2 * IDX_DIM + n_ssd_heads + IDX_HEADS
    n_small = -(-used // LANES) * LANES
    w_small = jnp.concatenate([col["ki"], col["ki"], col["dt"], col["wi"],
                               jnp.zeros((d, n_small - used), w_in.dtype)], axis=1).astype(BF16)
    dt_off = 2 * IDX_DIM
    wi_off = dt_off + n_ssd_heads

    proj = _matmul(hn, w_big, BF16, name="in_proj")
    small = _matmul(hn, w_small, F32, name="in_proj_small")

    qs, qis, kr, kid = _attn_prep(pos, proj, small, offs, idx_ln_g, idx_ln_b)
    o_attn = _dsa(qs, qis, small, kid, kr, proj, offs, bsz, s_len, wi_off)

    conv_out = _conv(proj, offs["xs"], conv_w, conv_b, s_len)
    y = _ssd(conv_out, proj, small, offs["z"], dt_off, dt_bias, a_log, d_skip, ssd_norm_g, bsz, s_len)

    merged = _merge(o_attn, w_attn_o.astype(BF16), y, w_ssd_o.astype(BF16), proj, offs["ga"], offs["gs"], d)
    mm = _matmul(merged, w_out.astype(BF16), F32, name="out_proj")
    return _post(mm, x, g_post, g_next, 1.0)


def kernel(x, positions, ffn1_pre_g, ffn1_w13, ffn1_w2, ffn1_post_g, mix_pre_g, w_in, idx_ln_g, idx_ln_b, conv_w, conv_b, dt_bias, a_log, d_skip, ssd_norm_g, w_attn_o, w_ssd_o, w_out, mix_post_g, ffn2_pre_g, ffn2_w13, ffn2_w2, ffn2_post_g):
    bsz, s_len, d = x.shape
    depth = ffn1_w13.shape[0]
    assert s_len % KEY_BLOCK == 0 and s_len % SSD_CHUNK == 0 and d % LANES == 0
    m = bsz * s_len
    xf = x.reshape(m, d)
    pos = positions.reshape(m, 1)
    hn = _prenorm(xf, ffn1_pre_g[0])
    for l in range(depth):
        xf, hn = _ffn(hn, xf, ffn1_w13[l], ffn1_w2[l], ffn1_post_g[l], mix_pre_g[l])
        xf, hn = _mixer(hn, xf, pos, w_in[l], idx_ln_g[l], idx_ln_b[l], conv_w[l], conv_b[l], dt_bias[l],
                        a_log[l], d_skip[l], ssd_norm_g[l], w_attn_o[l], w_ssd_o[l], w_out[l],
                        mix_post_g[l], ffn2_pre_g[l], bsz, s_len)
        g_next = ffn1_pre_g[l + 1] if l + 1 < depth else None
        xf, hn = _ffn(hn, xf, ffn2_w13[l], ffn2_w2[l], ffn2_post_g[l], g_next)
    return xf.reshape(bsz, s_len, d)
```

```python
import functools

import jax
import jax.numpy as jnp
import numpy as np
from jax import lax
from jax.experimental import pallas as pl
from jax.experimental.pallas import tpu as pltpu

N_HEADS = 16
N_KV_HEADS = 4
HEAD_DIM = 128
GQA = N_HEADS // N_KV_HEADS
D_ATTN = N_HEADS * HEAD_DIM
D_KV = N_KV_HEADS * HEAD_DIM
IDX_HEADS = 32
IDX_DIM = 64
D_IDX = IDX_HEADS * IDX_DIM
MAX_TOPK = 256
Q_BLOCK = 128
ROPE_THETA = 10000.0
SSD_HEAD_DIM = 64
SSD_GROUPS = 8
SSD_STATE = 128
CONV_WIDTH = 4
SSD_CHUNK = 128
EPS = 1e-6

LANES = 128
SUBLANES = 8
VMEM_LIMIT = 56 * 1024 * 1024

KEY_BLOCK = 256
IDX_HEAD_GROUP = 4
SSD_STEP_CHUNKS = 4
NEG_BIG = -1e30
INT_MIN = -(2 ** 31)

BF16 = jnp.bfloat16
F32 = jnp.float32


def _cparams(sem):
    return pltpu.CompilerParams(dimension_semantics=sem, vmem_limit_bytes=VMEM_LIMIT)


def _pick(n, pref):
    t = min(pref, n)
    while n % t:
        t //= 2
    return t


def _rms(v, g):
    return v * lax.rsqrt(jnp.mean(v * v, axis=-1, keepdims=True) + EPS) * g


def _prenorm_kernel(x_ref, g_ref, o_ref):
    o_ref[...] = _rms(x_ref[...], g_ref[...]).astype(o_ref.dtype)


def _prenorm(x, g):
    m, d = x.shape
    tr = _pick(m, 256)
    return pl.pallas_call(
        _prenorm_kernel,
        grid=(m // tr,),
        in_specs=[pl.BlockSpec((tr, d), lambda i: (i, 0)), pl.BlockSpec((1, d), lambda i: (0, 0))],
        out_specs=pl.BlockSpec((tr, d), lambda i: (i, 0)),
        out_shape=jax.ShapeDtypeStruct((m, d), BF16),
        compiler_params=_cparams(("parallel",)),
        name="prenorm",
    )(x, g.reshape(1, d))


def _post_kernel(m_ref, x_ref, gp_ref, gn_ref, xo_ref, hn_ref, *, coef):
    xn = x_ref[...] + coef * _rms(m_ref[...], gp_ref[...])
    xo_ref[...] = xn
    hn_ref[...] = _rms(xn, gn_ref[...]).astype(hn_ref.dtype)


def _post_last_kernel(m_ref, x_ref, gp_ref, xo_ref, *, coef):
    xo_ref[...] = x_ref[...] + coef * _rms(m_ref[...], gp_ref[...])


def _post(mm, x, g_post, g_next, coef):
    m, d = x.shape
    tr = _pick(m, 256)
    row = pl.BlockSpec((tr, d), lambda i: (i, 0))
    vec = pl.BlockSpec((1, d), lambda i: (0, 0))
    if g_next is None:
        return pl.pallas_call(
            functools.partial(_post_last_kernel, coef=coef),
            grid=(m // tr,),
            in_specs=[row, row, vec],
            out_specs=row,
            out_shape=jax.ShapeDtypeStruct((m, d), F32),
            compiler_params=_cparams(("parallel",)),
            name="post_last",
        )(mm, x, g_post.reshape(1, d)), None
    return pl.pallas_call(
        functools.partial(_post_kernel, coef=coef),
        grid=(m // tr,),
        in_specs=[row, row, vec, vec],
        out_specs=[row, row],
        out_shape=[jax.ShapeDtypeStruct((m, d), F32), jax.ShapeDtypeStruct((m, d), BF16)],
        compiler_params=_cparams(("parallel",)),
        name="post",
    )(mm, x, g_post.reshape(1, d), g_next.reshape(1, d))


def _cast_kernel(w_ref, o_ref):
    o_ref[...] = w_ref[...].astype(o_ref.dtype)


def _layer_bf16(w_stack, layer):
    _, r, c = w_stack.shape
    tr, tc = _pick(r, 1024), _pick(c, 2048)
    return pl.pallas_call(
        _cast_kernel,
        grid=(r // tr, c // tc),
        in_specs=[pl.BlockSpec((None, tr, tc), lambda i, j: (layer, i, j))],
        out_specs=pl.BlockSpec((tr, tc), lambda i, j: (i, j)),
        out_shape=jax.ShapeDtypeStruct((r, c), BF16),
        compiler_params=_cparams(("parallel", "parallel")),
        name="cast_w",
    )(w_stack)


def _w_in_kernel(w_ref, o_ref, *, segments):
    w = w_ref[...]
    for src, size, dst in segments:
        o_ref[:, dst:dst + size] = w[:, src:src + size].astype(o_ref.dtype)


def _w_in_bf16(w_stack, layer, segments, n_out):
    _, r, c = w_stack.shape
    tr = _pick(r, 128)
    return pl.pallas_call(
        functools.partial(_w_in_kernel, segments=segments),
        grid=(r // tr,),
        in_specs=[pl.BlockSpec((None, tr, c), lambda i: (layer, i, 0))],
        out_specs=pl.BlockSpec((tr, n_out), lambda i: (i, 0)),
        out_shape=jax.ShapeDtypeStruct((r, n_out), BF16),
        compiler_params=_cparams(("parallel",)),
        name="relayout_w_in",
    )(w_stack)


def _mm_kernel(x_ref, w_ref, o_ref, *scratch, nk):
    part = jnp.dot(x_ref[...], w_ref[...], preferred_element_type=F32)
    if nk == 1:
        o_ref[...] = part.astype(o_ref.dtype)
        return
    acc_ref, = scratch
    k = pl.program_id(2)

    @pl.when(k == 0)
    def _():
        acc_ref[...] = part

    @pl.when(k > 0)
    def _():
        acc_ref[...] += part

    @pl.when(k == nk - 1)
    def _():
        o_ref[...] = acc_ref[...].astype(o_ref.dtype)


def _matmul(x, w, out_dtype, tm=1024, tn=1024, tk=4096, name="matmul"):
    m, kd = x.shape
    n = w.shape[1]
    tm, tn, tk = _pick(m, tm), _pick(n, tn), _pick(kd, tk)
    nk = kd // tk
    return pl.pallas_call(
        functools.partial(_mm_kernel, nk=nk),
        grid=(m // tm, n // tn, nk),
        in_specs=[pl.BlockSpec((tm, tk), lambda i, j, k: (i, k)),
                  pl.BlockSpec((tk, tn), lambda i, j, k: (k, j))],
        out_specs=pl.BlockSpec((tm, tn), lambda i, j, k: (i, j)),
        out_shape=jax.ShapeDtypeStruct((m, n), out_dtype),
        scratch_shapes=[pltpu.VMEM((tm, tn), F32)] if nk > 1 else [],
        compiler_params=_cparams(("parallel", "parallel", "arbitrary")),
        name=name,
    )(x, w)


def _swiglu_kernel(x_ref, w1_ref, w3_ref, o_ref):
    x = x_ref[...]
    a = jnp.dot(x, w1_ref[...], preferred_element_type=F32)
    b = jnp.dot(x, w3_ref[...], preferred_element_type=F32)
    o_ref[...] = (a * jax.nn.sigmoid(a) * b).astype(o_ref.dtype)


def _swiglu_up(hn, w13):
    m, kd = hn.shape
    f = w13.shape[1] // 2
    tm, tn = _pick(m, 1024), _pick(f, 512)
    nj = f // tn
    return pl.pallas_call(
        _swiglu_kernel,
        grid=(m // tm, nj),
        in_specs=[pl.BlockSpec((tm, kd), lambda i, j: (i, 0)),
                  pl.BlockSpec((kd, tn), lambda i, j: (0, j)),
                  pl.BlockSpec((kd, tn), lambda i, j: (0, j + nj))],
        out_specs=pl.BlockSpec((tm, tn), lambda i, j: (i, j)),
        out_shape=jax.ShapeDtypeStruct((m, f), BF16),
        compiler_params=_cparams(("parallel", "parallel")),
        name="swiglu_up",
    )(hn, w13, w13)


def _merge_kernel(o_ref, wa_ref, y_ref, ws_ref, ga_ref, gs_ref, out_ref):
    a = jnp.dot(o_ref[...], wa_ref[...], preferred_element_type=F32)
    s = jnp.dot(y_ref[...], ws_ref[...], preferred_element_type=F32)
    ga = jax.nn.sigmoid(ga_ref[...].astype(F32))
    gs = jax.nn.sigmoid(gs_ref[...].astype(F32))
    out_ref[...] = (ga * a + gs * s).astype(out_ref.dtype)


def _merge(o_attn, wa, y, ws, proj, ga_off, gs_off, d):
    m = o_attn.shape[0]
    tm, tn = _pick(m, 1024), _pick(d, 512)
    ja, js = ga_off // tn, gs_off // tn
    return pl.pallas_call(
        _merge_kernel,
        grid=(m // tm, d // tn),
        in_specs=[pl.BlockSpec((tm, o_attn.shape[1]), lambda i, j: (i, 0)),
                  pl.BlockSpec((wa.shape[0], tn), lambda i, j: (0, j)),
                  pl.BlockSpec((tm, y.shape[1]), lambda i, j: (i, 0)),
                  pl.BlockSpec((ws.shape[0], tn), lambda i, j: (0, j)),
                  pl.BlockSpec((tm, tn), lambda i, j: (i, ja + j)),
                  pl.BlockSpec((tm, tn), lambda i, j: (i, js + j))],
        out_specs=pl.BlockSpec((tm, tn), lambda i, j: (i, j)),
        out_shape=jax.ShapeDtypeStruct((m, d), BF16),
        compiler_params=_cparams(("parallel", "parallel")),
        name="merge",
    )(o_attn, wa, y, ws, proj, proj)


def _rope_tables(pos, invf, sign):
    ang = pos * invf
    return jnp.cos(ang), jnp.sin(ang) * sign


def _rope128(x, cos, sin_signed):
    return x * cos + pltpu.roll(x, HEAD_DIM // 2, 1) * sin_signed


def _rope64(x, cos, sin_signed, first_half):
    half = IDX_DIM // 2
    rot = jnp.where(first_half, pltpu.roll(x, LANES - half, 1), pltpu.roll(x, half, 1))
    return x * cos + rot * sin_signed


def _attn_prep_kernel(pos_ref, q_ref, qi_ref, k_ref, sm_ref, f128_ref, s128_ref, f64_ref, s64_ref,
                      lng_ref, lnb_ref, qs_ref, qis_ref, kr_ref, kid_ref):
    pos = pos_ref[...].astype(F32)
    cos_a, sin_a = _rope_tables(pos, f128_ref[...], s128_ref[...])
    cos_b, sin_b = _rope_tables(pos, f64_ref[...], s64_ref[...])
    lane = lax.broadcasted_iota(jnp.int32, (Q_BLOCK, LANES), 1)
    first_half = (lane & (IDX_DIM - 1)) < IDX_DIM // 2
    low_head = lane < IDX_DIM
    scale = HEAD_DIM ** -0.5
    for h in range(N_HEADS):
        xh = q_ref[:, h * HEAD_DIM:(h + 1) * HEAD_DIM].astype(F32)
        r = _rope128(xh, cos_a, sin_a) * scale
        hk, g = divmod(h, GQA)
        qs_ref[0, hk, g * Q_BLOCK:(g + 1) * Q_BLOCK, :] = r.astype(qs_ref.dtype)
    for h in range(N_KV_HEADS):
        xh = k_ref[:, h * HEAD_DIM:(h + 1) * HEAD_DIM].astype(F32)
        kr_ref[:, h * HEAD_DIM:(h + 1) * HEAD_DIM] = _rope128(xh, cos_a, sin_a).astype(kr_ref.dtype)
    zero = jnp.zeros((Q_BLOCK, LANES), F32)
    for j in range(IDX_HEADS // 2):
        xp = qi_ref[:, j * LANES:(j + 1) * LANES].astype(F32)
        r = _rope64(xp, cos_b, sin_b, first_half)
        qis_ref[0, 2 * j] = jnp.where(low_head, r, zero).astype(qis_ref.dtype)
        qis_ref[0, 2 * j + 1] = jnp.where(low_head, zero, r).astype(qis_ref.dtype)
    ki = sm_ref[:, 0:LANES]
    mu = jnp.mean(ki, axis=-1, keepdims=True)
    var = jnp.mean(jnp.square(ki - mu), axis=-1, keepdims=True)
    kn = (ki - mu) * lax.rsqrt(var + EPS) * lng_ref[...] + lnb_ref[...]
    kid_ref[...] = _rope64(kn, cos_b, sin_b, first_half).astype(kid_ref.dtype)


def _attn_prep(pos, proj, small, offs, idx_ln_g, idx_ln_b):
    m = proj.shape[0]
    nblk = m // Q_BLOCK
    half_a, half_b = HEAD_DIM // 2, IDX_DIM // 2
    inv_a = 1.0 / (ROPE_THETA ** (jnp.arange(half_a, dtype=F32) * (2.0 / HEAD_DIM)))
    inv_b = 1.0 / (ROPE_THETA ** (jnp.arange(half_b, dtype=F32) * (2.0 / IDX_DIM)))
    f128 = jnp.tile(inv_a, 2).reshape(1, LANES)
    f64 = jnp.tile(inv_b, 4).reshape(1, LANES)
    s128 = jnp.asarray(np.repeat([-1.0, 1.0], half_a), F32).reshape(1, LANES)
    s64 = jnp.asarray(np.tile(np.repeat([-1.0, 1.0], half_b), 2), F32).reshape(1, LANES)
    lng = jnp.tile(idx_ln_g, 2).reshape(1, LANES)
    lnb = jnp.tile(idx_ln_b, 2).reshape(1, LANES)
    vec = pl.BlockSpec((1, LANES), lambda i: (0, 0))
    return pl.pallas_call(
        _attn_prep_kernel,
        grid=(nblk,),
        in_specs=[pl.BlockSpec((Q_BLOCK, 1), lambda i: (i, 0)),
                  pl.BlockSpec((Q_BLOCK, D_ATTN), lambda i: (i, offs["q"] // D_ATTN)),
                  pl.BlockSpec((Q_BLOCK, D_IDX), lambda i: (i, offs["qi"] // D_IDX)),
                  pl.BlockSpec((Q_BLOCK, D_KV), lambda i: (i, offs["k"] // D_KV)),
                  pl.BlockSpec((Q_BLOCK, small.shape[1]), lambda i: (i, 0)),
                  vec, vec, vec, vec, vec, vec],
        out_specs=[pl.BlockSpec((1, N_KV_HEADS, GQA * Q_BLOCK, HEAD_DIM), lambda i: (i, 0, 0, 0)),
                   pl.BlockSpec((1, IDX_HEADS, Q_BLOCK, LANES), lambda i: (i, 0, 0, 0)),
                   pl.BlockSpec((Q_BLOCK, D_KV), lambda i: (i, 0)),
                   pl.BlockSpec((Q_BLOCK, LANES), lambda i: (i, 0))],
        out_shape=[jax.ShapeDtypeStruct((nblk, N_KV_HEADS, GQA * Q_BLOCK, HEAD_DIM), BF16),
                   jax.ShapeDtypeStruct((nblk, IDX_HEADS, Q_BLOCK, LANES), BF16),
                   jax.ShapeDtypeStruct((m, D_KV), BF16),
                   jax.ShapeDtypeStruct((m, LANES), BF16)],
        compiler_params=_cparams(("parallel",)),
        name="attn_prep",
    )(pos, proj, proj, proj, small, f128, s128, f64, s64, lng, lnb)


def _dsa_kernel(qs_ref, qis_ref, sm_ref, kid_ref, kr_ref, v_ref, o_ref,
                wb_scr, key_scr, bias_scr, m_scr, l_scr, acc_scr, *, topk, wi_off):
    i = pl.program_id(1)
    nkb = (i * Q_BLOCK + Q_BLOCK + KEY_BLOCK - 1) // KEY_BLOCK
    nsub = KEY_BLOCK // LANES
    wi_scale = IDX_HEADS ** -0.5 * IDX_DIM ** -0.5

    for h in range(IDX_HEADS):
        col = sm_ref[:, wi_off + h:wi_off + h + 1] * wi_scale
        wb_scr[h] = jnp.broadcast_to(col, (Q_BLOCK, LANES))

    row_pos = i * Q_BLOCK + lax.broadcasted_iota(jnp.int32, (Q_BLOCK, KEY_BLOCK), 0)
    col_iota = lax.broadcasted_iota(jnp.int32, (Q_BLOCK, KEY_BLOCK), 1)

    def score_block(kb, carry):
        k0 = pl.multiple_of(kb * KEY_BLOCK, KEY_BLOCK)
        ki = kid_ref[pl.ds(k0, KEY_BLOCK), :]
        score = jnp.zeros((Q_BLOCK, KEY_BLOCK), F32)
        for hg in range(IDX_HEADS // IDX_HEAD_GROUP):
            h0 = hg * IDX_HEAD_GROUP
            q2 = qis_ref[0, h0:h0 + IDX_HEAD_GROUP].reshape(IDX_HEAD_GROUP * Q_BLOCK, LANES)
            d = lax.dot_general(q2, ki, (((1,), (1,)), ((), ())), preferred_element_type=F32)
            for u in range(IDX_HEAD_GROUP):
                w = wb_scr[h0 + u]
                score = score + jnp.maximum(d[u * Q_BLOCK:(u + 1) * Q_BLOCK], 0.0) * jnp.concatenate([w] * nsub, axis=1)
        bits = lax.bitcast_convert_type(score, jnp.int32)
        skey = bits ^ ((bits >> 31) & jnp.int32(0x7FFFFFFF))
        causal = (k0 + col_iota) <= row_pos
        key_scr[kb] = jnp.where(causal, skey, jnp.int32(INT_MIN))
        return carry

    lax.fori_loop(0, nkb, score_block, 0)

    def bit_step(it, thr):
        cand = thr ^ lax.shift_left(jnp.int32(1), 31 - it)
        cand_b = jnp.broadcast_to(cand, (Q_BLOCK, LANES))

        def count_block(kb, acc):
            blk = key_scr[kb]
            for c in range(nsub):
                acc = acc + jnp.where(blk[:, c * LANES:(c + 1) * LANES] >= cand_b, 1, 0)
            return acc

        cnt = lax.fori_loop(0, nkb, count_block, jnp.zeros((Q_BLOCK, LANES), jnp.int32))
        total = jnp.sum(cnt, axis=1, keepdims=True)
        return jnp.where(total >= topk, cand, thr)

    thr = lax.fori_loop(0, 32, bit_step, jnp.full((Q_BLOCK, 1), INT_MIN, jnp.int32))
    thr = jnp.maximum(thr, jnp.int32(INT_MIN + 1))
    thr_b = jnp.broadcast_to(thr, (Q_BLOCK, KEY_BLOCK))

    def bias_block(kb, carry):
        bias_scr[kb] = jnp.where(key_scr[kb] >= thr_b, 0.0, NEG_BIG).astype(F32)
        return carry

    lax.fori_loop(0, nkb, bias_block, 0)

    def masked_logits(kb, hk):
        k0 = pl.multiple_of(kb * KEY_BLOCK, KEY_BLOCK)
        kblk = kr_ref[pl.ds(k0, KEY_BLOCK), hk * HEAD_DIM:(hk + 1) * HEAD_DIM]
        s = lax.dot_general(qs_ref[0, hk], kblk, (((1,), (1,)), ((), ())), preferred_element_type=F32)
        return s + jnp.concatenate([bias_scr[kb]] * GQA, axis=0)

    m_scr[...] = jnp.full(m_scr.shape, NEG_BIG, F32)

    def max_block(kb, carry):
        for hk in range(N_KV_HEADS):
            s = masked_logits(kb, hk)
            part = s[:, 0:LANES]
            for c in range(1, nsub):
                part = jnp.maximum(part, s[:, c * LANES:(c + 1) * LANES])
            m_scr[hk] = jnp.maximum(m_scr[hk], part)
        return carry

    lax.fori_loop(0, nkb, max_block, 0)
    for hk in range(N_KV_HEADS):
        row_max = jnp.max(m_scr[hk], axis=1, keepdims=True)
        m_scr[hk] = jnp.broadcast_to(row_max, m_scr.shape[1:])
    l_scr[...] = jnp.zeros(l_scr.shape, F32)
    acc_scr[...] = jnp.zeros(acc_scr.shape, F32)

    def pv_block(kb, carry):
        k0 = pl.multiple_of(kb * KEY_BLOCK, KEY_BLOCK)
        for hk in range(N_KV_HEADS):
            vblk = v_ref[pl.ds(k0, KEY_BLOCK), hk * HEAD_DIM:(hk + 1) * HEAD_DIM]
            p = jnp.exp(masked_logits(kb, hk) - jnp.concatenate([m_scr[hk]] * nsub, axis=1))
            part = p[:, 0:LANES]
            for c in range(1, nsub):
                part = part + p[:, c * LANES:(c + 1) * LANES]
            l_scr[hk] += part
            acc_scr[hk] += jnp.dot(p.astype(BF16), vblk, preferred_element_type=F32)
        return carry

    lax.fori_loop(0, nkb, pv_block, 0)
    for hk in range(N_KV_HEADS):
        out = acc_scr[hk] / jnp.sum(l_scr[hk], axis=1, keepdims=True)
        for g in range(GQA):
            h = hk * GQA + g
            o_ref[:, h * HEAD_DIM:(h + 1) * HEAD_DIM] = out[g * Q_BLOCK:(g + 1) * Q_BLOCK].astype(o_ref.dtype)


def _dsa(qs, qis, small, kid, kr, proj, offs, bsz, s_len, wi_off):
    nblk = s_len // Q_BLOCK
    nkb = s_len // KEY_BLOCK
    topk = min(MAX_TOPK, s_len // 4)
    vcol = offs["v"] // D_KV
    return pl.pallas_call(
        functools.partial(_dsa_kernel, topk=topk, wi_off=wi_off),
        grid=(bsz, nblk),
        in_specs=[pl.BlockSpec((1, N_KV_HEADS, GQA * Q_BLOCK, HEAD_DIM), lambda b, i: (b * nblk + i, 0, 0, 0)),
                  pl.BlockSpec((1, IDX_HEADS, Q_BLOCK, LANES), lambda b, i: (b * nblk + i, 0, 0, 0)),
                  pl.BlockSpec((Q_BLOCK, small.shape[1]), lambda b, i: (b * nblk + i, 0)),
                  pl.BlockSpec((s_len, LANES), lambda b, i: (b, 0)),
                  pl.BlockSpec((s_len, D_KV), lambda b, i: (b, 0)),
                  pl.BlockSpec((s_len, D_KV), lambda b, i: (b, vcol))],
        out_specs=pl.BlockSpec((Q_BLOCK, D_ATTN), lambda b, i: (b * nblk + i, 0)),
        out_shape=jax.ShapeDtypeStruct((bsz * s_len, D_ATTN), BF16),
        scratch_shapes=[pltpu.VMEM((IDX_HEADS, Q_BLOCK, LANES), F32),
                        pltpu.VMEM((nkb, Q_BLOCK, KEY_BLOCK), jnp.int32),
                        pltpu.VMEM((nkb, Q_BLOCK, KEY_BLOCK), F32),
                        pltpu.VMEM((N_KV_HEADS, GQA * Q_BLOCK, LANES), F32),
                        pltpu.VMEM((N_KV_HEADS, GQA * Q_BLOCK, LANES), F32),
                        pltpu.VMEM((N_KV_HEADS, GQA * Q_BLOCK, HEAD_DIM), F32)],
        compiler_params=_cparams(("parallel", "arbitrary")),
        name="dsa",
    )(qs, qis, small, kid, kr, proj)


def _conv_kernel(x_ref, halo_ref, w_ref, b_ref, o_ref, *, tiles_per_seq):
    st = x_ref.shape[0]
    first = (pl.program_id(0) % tiles_per_seq) == 0
    halo = jnp.where(first, 0.0, halo_ref[...].astype(F32))
    xx = jnp.concatenate([halo, x_ref[...].astype(F32)], axis=0)
    acc = b_ref[...] + w_ref[CONV_WIDTH - 1:CONV_WIDTH, :] * xx[SUBLANES:, :]
    for k in range(CONV_WIDTH - 1):
        shifted = pltpu.roll(xx, CONV_WIDTH - 1 - k, 0)[SUBLANES:, :]
        acc = acc + w_ref[k:k + 1, :] * shifted
    o_ref[...] = (acc * jax.nn.sigmoid(acc)).astype(o_ref.dtype)


def _conv(proj, xs_off, conv_w, conv_b, s_len):
    m = proj.shape[0]
    ch = conv_w.shape[1]
    st, ct = _pick(s_len, 512), _pick(ch, 512)
    c0 = xs_off // ct
    hb = st // SUBLANES
    return pl.pallas_call(
        functools.partial(_conv_kernel, tiles_per_seq=s_len // st),
        grid=(m // st, ch // ct),
        in_specs=[pl.BlockSpec((st, ct), lambda r, c: (r, c0 + c)),
                  pl.BlockSpec((SUBLANES, ct), lambda r, c: (jnp.maximum(r * hb - 1, 0), c0 + c)),
                  pl.BlockSpec((CONV_WIDTH, ct), lambda r, c: (0, c)),
                  pl.BlockSpec((1, ct), lambda r, c: (0, c))],
        out_specs=pl.BlockSpec((st, ct), lambda r, c: (r, c)),
        out_shape=jax.ShapeDtypeStruct((m, ch), BF16),
        compiler_params=_cparams(("parallel", "parallel")),
        name="conv",
    )(proj, proj, conv_w, conv_b.reshape(1, ch))


def _softplus(x):
    return jnp.maximum(x, 0.0) + jnp.log1p(jnp.exp(-jnp.abs(x)))


def _split_bf16(v, parts):
    out = []
    for _ in range(parts):
        p = v.astype(BF16)
        out.append(p)
        v = v - p.astype(F32)
    return out


def _ssd_kernel(x_ref, b_ref, c_ref, z_ref, dtc_ref, pc_ref, dsk_ref, ng_ref, o_ref, state_scr, *, heads):
    gw = heads * SSD_HEAD_DIM
    rep = LANES // heads
    cl = SSD_CHUNK

    @pl.when(pl.program_id(2) == 0)
    def _():
        state_scr[...] = jnp.zeros(state_scr.shape, F32)

    li = lax.broadcasted_iota(jnp.int32, (cl, cl), 0)
    si = lax.broadcasted_iota(jnp.int32, (cl, cl), 1)
    tri = li >= si
    tri_b = tri.astype(BF16)
    er = lax.broadcasted_iota(jnp.int32, (LANES, gw), 0)
    ec = lax.broadcasted_iota(jnp.int32, (LANES, gw), 1)
    expand = (er == lax.shift_right_logical(ec, SSD_HEAD_DIM.bit_length() - 1) * rep).astype(BF16)
    lane = lax.broadcasted_iota(jnp.int32, (cl, LANES), 1)
    neg_a = -jnp.exp(pc_ref[0, 1:2, :])

    state = state_scr[...]
    for sc in range(x_ref.shape[0] // cl):
        rows = slice(sc * cl, (sc + 1) * cl)
        dtc = _softplus(dtc_ref[0, 0, rows, :] + pc_ref[0, 0:1, :])
        adt_c = dtc * neg_a
        acs_c = sum(jnp.dot(tri_b, p, preferred_element_type=F32) for p in _split_bf16(adt_c, 3))
        acs_r = acs_c.T
        last = acs_c[cl - 1:cl, :]
        dec_hi, dec_lo = _split_bf16(jnp.exp(acs_c), 2)
        stacked = jnp.concatenate([dtc.astype(BF16), (dtc * jnp.exp(last - acs_c)).astype(BF16), dec_hi, dec_lo],
                                  axis=0)
        ex = jnp.dot(stacked, expand, preferred_element_type=F32)
        dt_e, dts_e, dec_e = ex[0:cl], ex[cl:2 * cl], ex[2 * cl:3 * cl] + ex[3 * cl:4 * cl]

        xg = x_ref[rows, :].astype(F32)
        xdt = (xg * dt_e).astype(BF16)
        xst = (xg * dts_e).astype(BF16)
        bm = b_ref[rows, :]
        cm = c_ref[rows, :]
        cb = lax.dot_general(cm, bm, (((1,), (1,)), ((), ())), preferred_element_type=F32)
        pieces = []
        for j in range(gw // LANES):
            ys = []
            for r in (2 * j, 2 * j + 1):
                col = jnp.broadcast_to(acs_c[:, r * rep:r * rep + 1], (cl, cl))
                row = jnp.broadcast_to(acs_r[r * rep:r * rep + 1, :], (cl, cl))
                lmat = jnp.where(tri, jnp.exp(col - row), 0.0)
                ys.append(jnp.dot((cb * lmat).astype(BF16), xdt[:, j * LANES:(j + 1) * LANES],
                                  preferred_element_type=F32))
            pieces.append(jnp.where(lane < SSD_HEAD_DIM, ys[0], ys[1]))
        y_diag = jnp.concatenate(pieces, axis=1) if len(pieces) > 1 else pieces[0]

        y_off = jnp.dot(cm, state.astype(BF16), preferred_element_type=F32) * dec_e
        y = y_diag + y_off + xg * dsk_ref[0]
        state = state * dec_e[cl - 1:cl, :] + lax.dot_general(
            bm, xst, (((0,), (0,)), ((), ())), preferred_element_type=F32)

        z = z_ref[rows, :].astype(F32)
        yz = y * (z * jax.nn.sigmoid(z))
        o_ref[rows, :] = _rms(yz, ng_ref[...]).astype(o_ref.dtype)
    state_scr[...] = state


def _ssd(conv_out, proj, small, z_off, dt_off, dt_bias, a_log, d_skip, norm_g, bsz, s_len):
    d_ssd = norm_g.shape[0]
    n_heads = a_log.shape[0]
    heads = n_heads // SSD_GROUPS
    gw = heads * SSD_HEAD_DIM
    rep = LANES // heads
    m = bsz * s_len
    dt = small[:, dt_off:dt_off + n_heads].reshape(bsz, s_len, SSD_GROUPS, heads)
    dtc = jnp.transpose(jnp.repeat(dt, rep, axis=-1), (0, 2, 1, 3))
    par = jnp.repeat(jnp.stack([dt_bias, a_log]).reshape(2, SSD_GROUPS, heads), rep, axis=-1)
    pc = jnp.transpose(par, (1, 0, 2))
    dsk = jnp.repeat(d_skip, SSD_HEAD_DIM).reshape(SSD_GROUPS, 1, gw)
    b0 = d_ssd // SSD_STATE
    c0 = b0 + SSD_GROUPS
    z0 = z_off // gw
    rows = SSD_STEP_CHUNKS * SSD_CHUNK
    assert s_len % rows == 0
    nc = s_len // rows
    return pl.pallas_call(
        functools.partial(_ssd_kernel, heads=heads),
        grid=(bsz, SSD_GROUPS, nc),
        in_specs=[pl.BlockSpec((rows, gw), lambda b, g, c: (b * nc + c, g)),
                  pl.BlockSpec((rows, SSD_STATE), lambda b, g, c: (b * nc + c, b0 + g)),
                  pl.BlockSpec((rows, SSD_STATE), lambda b, g, c: (b * nc + c, c0 + g)),
                  pl.BlockSpec((rows, gw), lambda b, g, c: (b * nc + c, z0 + g)),
                  pl.BlockSpec((1, 1, rows, LANES), lambda b, g, c: (b, g, c, 0)),
                  pl.BlockSpec((1, 2, LANES), lambda b, g, c: (g, 0, 0)),
                  pl.BlockSpec((1, 1, gw), lambda b, g, c: (g, 0, 0)),
                  pl.BlockSpec((1, gw), lambda b, g, c: (0, g))],
        out_specs=pl.BlockSpec((rows, gw), lambda b, g, c: (b * nc + c, g)),
        out_shape=jax.ShapeDtypeStruct((m, d_ssd), BF16),
        scratch_shapes=[pltpu.VMEM((SSD_STATE, gw), F32)],
        compiler_params=_cparams(("parallel", "parallel", "arbitrary")),
        name="ssd",
    )(conv_out, conv_out, conv_out, proj, dtc, pc, dsk, norm_g.reshape(1, d_ssd))


def _ffn(hn, x, w13_stack, w2_stack, layer, g_post, g_next):
    h = _swiglu_up(hn, _layer_bf16(w13_stack, layer))
    mm = _matmul(h, _layer_bf16(w2_stack, layer), F32, name="ffn_down")
    return _post(mm, x, g_post, g_next, 0.5)


def _mixer(hn, x, pos, w_in_stack, idx_ln_g, idx_ln_b, conv_w, conv_b, dt_bias, a_log, d_skip, ssd_norm_g,
           wa_stack, ws_stack, wo_stack, layer, g_post, g_next, bsz, s_len):
    d = x.shape[1]
    d_ssd = ssd_norm_g.shape[0]
    n_ssd_heads = a_log.shape[0]
    gn = SSD_GROUPS * SSD_STATE
    sizes = (D_ATTN, D_KV, D_KV, D_IDX, IDX_DIM, IDX_HEADS, d_ssd, d_ssd, gn, gn, n_ssd_heads, d, d)
    names = ("q", "k", "v", "qi", "ki", "wi", "z", "xs", "bs", "cs", "dt", "ga", "gs")
    starts = {nm: int(s0) for nm, s0 in zip(names, np.concatenate([[0], np.cumsum(sizes)[:-1]]))}
    size = dict(zip(names, sizes))
    order = ("q", "qi", "k", "v", "z", "xs", "bs", "cs", "ga", "gs")
    offs, o = {}, 0
    for nm in order:
        offs[nm] = o
        o += size[nm]
    w_big = _w_in_bf16(w_in_stack, layer, tuple((starts[nm], size[nm], offs[nm]) for nm in order), o)
    w_in = w_in_stack[layer]
    col = {nm: w_in[:, starts[nm]:starts[nm] + size[nm]] for nm in ("ki", "dt", "wi")}
    used = 2 * IDX_DIM + n_ssd_heads + IDX_HEADS
    n_small = -(-used // LANES) * LANES
    w_small = jnp.concatenate([col["ki"], col["ki"], col["dt"], col["wi"],
                               jnp.zeros((d, n_small - used), w_in.dtype)], axis=1).astype(BF16)
    dt_off = 2 * IDX_DIM
    wi_off = dt_off + n_ssd_heads

    proj = _matmul(hn, w_big, BF16, name="in_proj")
    small = _matmul(hn, w_small, F32, name="in_proj_small")

    qs, qis, kr, kid = _attn_prep(pos, proj, small, offs, idx_ln_g, idx_ln_b)
    o_attn = _dsa(qs, qis, small, kid, kr, proj, offs, bsz, s_len, wi_off)

    conv_out = _conv(proj, offs["xs"], conv_w, conv_b, s_len)
    y = _ssd(conv_out, proj, small, offs["z"], dt_off, dt_bias, a_log, d_skip, ssd_norm_g, bsz, s_len)

    merged = _merge(o_attn, _layer_bf16(wa_stack, layer), y, _layer_bf16(ws_stack, layer), proj,
                    offs["ga"], offs["gs"], d)
    mm = _matmul(merged, _layer_bf16(wo_stack, layer), F32, name="out_proj")
    return _post(mm, x, g_post, g_next, 1.0)


def kernel(x, positions, ffn1_pre_g, ffn1_w13, ffn1_w2, ffn1_post_g, mix_pre_g, w_in, idx_ln_g, idx_ln_b, conv_w, conv_b, dt_bias, a_log, d_skip, ssd_norm_g, w_attn_o, w_ssd_o, w_out, mix_post_g, ffn2_pre_g, ffn2_w13, ffn2_w2, ffn2_post_g):
    bsz, s_len, d = x.shape
    depth = ffn1_w13.shape[0]
    assert s_len % KEY_BLOCK == 0 and s_len % SSD_CHUNK == 0 and d % LANES == 0
    m = bsz * s_len
    xf = x.reshape(m, d)
    pos = positions.reshape(m, 1)
    hn = _prenorm(xf, ffn1_pre_g[0])
    for l in range(depth):
        xf, hn = _ffn(hn, xf, ffn1_w13, ffn1_w2, l, ffn1_post_g[l], mix_pre_g[l])
        xf, hn = _mixer(hn, xf, pos, w_in, idx_ln_g[l], idx_ln_b[l], conv_w[l], conv_b[l], dt_bias[l],
                        a_log[l], d_skip[l], ssd_norm_g[l], w_attn_o, w_ssd_o, w_out, l,
                        mix_post_g[l], ffn2_pre_g[l], bsz, s_len)
        g_next = ffn1_pre_g[l + 1] if l + 1 < depth else None
        xf, hn = _ffn(hn, xf, ffn2_w13, ffn2_w2, l, ffn2_post_g[l], g_next)
    return xf.reshape(bsz, s_len, d)
```

```python
import functools

import jax
import jax.numpy as jnp
import numpy as np
from jax import lax
from jax.experimental import pallas as pl
from jax.experimental.pallas import tpu as pltpu

N_HEADS = 16
N_KV_HEADS = 4
HEAD_DIM = 128
GQA = N_HEADS // N_KV_HEADS
D_ATTN = N_HEADS * HEAD_DIM
D_KV = N_KV_HEADS * HEAD_DIM
IDX_HEADS = 32
IDX_DIM = 64
D_IDX = IDX_HEADS * IDX_DIM
MAX_TOPK = 256
Q_BLOCK = 128
ROPE_THETA = 10000.0
SSD_HEAD_DIM = 64
SSD_GROUPS = 8
SSD_STATE = 128
CONV_WIDTH = 4
SSD_CHUNK = 128
EPS = 1e-6

LANES = 128
SUBLANES = 8
VMEM_LIMIT = 56 * 1024 * 1024

KEY_BLOCK = 256
IDX_HEAD_GROUP = 4
ATTN_UNROLL = 2
SSD_STEP_CHUNKS = 4
NEG_BIG = -1e30
INT_MIN = -(2 ** 31)

BF16 = jnp.bfloat16
F32 = jnp.float32


def _cparams(sem):
    return pltpu.CompilerParams(dimension_semantics=sem, vmem_limit_bytes=VMEM_LIMIT)


def _pick(n, pref):
    t = min(pref, n)
    while n % t:
        t //= 2
    return t


def _rms(v, g):
    return v * lax.rsqrt(jnp.mean(v * v, axis=-1, keepdims=True) + EPS) * g


def _prenorm_kernel(x_ref, g_ref, o_ref):
    o_ref[...] = _rms(x_ref[...], g_ref[...]).astype(o_ref.dtype)


def _prenorm(x, g):
    m, d = x.shape
    tr = _pick(m, 256)
    return pl.pallas_call(
        _prenorm_kernel,
        grid=(m // tr,),
        in_specs=[pl.BlockSpec((tr, d), lambda i: (i, 0)), pl.BlockSpec((1, d), lambda i: (0, 0))],
        out_specs=pl.BlockSpec((tr, d), lambda i: (i, 0)),
        out_shape=jax.ShapeDtypeStruct((m, d), BF16),
        compiler_params=_cparams(("parallel",)),
        name="prenorm",
    )(x, g.reshape(1, d))


def _post_kernel(m_ref, x_ref, gp_ref, gn_ref, xo_ref, hn_ref, *, coef):
    xn = x_ref[...] + coef * _rms(m_ref[...], gp_ref[...])
    xo_ref[...] = xn
    hn_ref[...] = _rms(xn, gn_ref[...]).astype(hn_ref.dtype)


def _post_last_kernel(m_ref, x_ref, gp_ref, xo_ref, *, coef):
    xo_ref[...] = x_ref[...] + coef * _rms(m_ref[...], gp_ref[...])


def _post(mm, x, g_post, g_next, coef):
    m, d = x.shape
    tr = _pick(m, 256)
    row = pl.BlockSpec((tr, d), lambda i: (i, 0))
    vec = pl.BlockSpec((1, d), lambda i: (0, 0))
    if g_next is None:
        return pl.pallas_call(
            functools.partial(_post_last_kernel, coef=coef),
            grid=(m // tr,),
            in_specs=[row, row, vec],
            out_specs=row,
            out_shape=jax.ShapeDtypeStruct((m, d), F32),
            compiler_params=_cparams(("parallel",)),
            name="post_last",
        )(mm, x, g_post.reshape(1, d)), None
    return pl.pallas_call(
        functools.partial(_post_kernel, coef=coef),
        grid=(m // tr,),
        in_specs=[row, row, vec, vec],
        out_specs=[row, row],
        out_shape=[jax.ShapeDtypeStruct((m, d), F32), jax.ShapeDtypeStruct((m, d), BF16)],
        compiler_params=_cparams(("parallel",)),
        name="post",
    )(mm, x, g_post.reshape(1, d), g_next.reshape(1, d))


def _cast_kernel(w_ref, o_ref):
    o_ref[...] = w_ref[...].astype(o_ref.dtype)


def _layer_bf16(w_stack, layer):
    _, r, c = w_stack.shape
    tr, tc = _pick(r, 1024), _pick(c, 2048)
    return pl.pallas_call(
        _cast_kernel,
        grid=(r // tr, c // tc),
        in_specs=[pl.BlockSpec((None, tr, tc), lambda i, j: (layer, i, j))],
        out_specs=pl.BlockSpec((tr, tc), lambda i, j: (i, j)),
        out_shape=jax.ShapeDtypeStruct((r, c), BF16),
        compiler_params=_cparams(("parallel", "parallel")),
        name="cast_w",
    )(w_stack)


def _w_in_kernel(w_ref, big_ref, small_ref, *, big_segments, small_segments):
    w = w_ref[...]
    for src, size, dst in big_segments:
        big_ref[:, dst:dst + size] = w[:, src:src + size]
    end = 0
    for src, size, dst in small_segments:
        small_ref[:, dst:dst + size] = w[:, src:src + size]
        end = max(end, dst + size)
    if end < small_ref.shape[1]:
        small_ref[:, end:] = jnp.zeros((small_ref.shape[0], small_ref.shape[1] - end), small_ref.dtype)


def _w_in_relayout(w_stack, layer, big_segments, n_big, small_segments, n_small):
    _, r, c = w_stack.shape
    tr = _pick(r, 128)
    return pl.pallas_call(
        functools.partial(_w_in_kernel, big_segments=big_segments, small_segments=small_segments),
        grid=(r // tr,),
        in_specs=[pl.BlockSpec((None, tr, c), lambda i: (layer, i, 0))],
        out_specs=[pl.BlockSpec((tr, n_big), lambda i: (i, 0)), pl.BlockSpec((tr, n_small), lambda i: (i, 0))],
        out_shape=[jax.ShapeDtypeStruct((r, n_big), w_stack.dtype), jax.ShapeDtypeStruct((r, n_small), w_stack.dtype)],
        compiler_params=_cparams(("parallel",)),
        name="relayout_w_in",
    )(w_stack)


def _mm_kernel(x_ref, w_ref, o_ref, *scratch, nk):
    part = jnp.dot(x_ref[...], w_ref[...], preferred_element_type=F32)
    if nk == 1:
        o_ref[...] = part.astype(o_ref.dtype)
        return
    acc_ref, = scratch
    k = pl.program_id(2)

    @pl.when(k == 0)
    def _():
        acc_ref[...] = part

    @pl.when(k > 0)
    def _():
        acc_ref[...] += part

    @pl.when(k == nk - 1)
    def _():
        o_ref[...] = acc_ref[...].astype(o_ref.dtype)


def _matmul(x, w, out_dtype, tm=1024, tn=1024, tk=4096, name="matmul"):
    m, kd = x.shape
    n = w.shape[1]
    tm, tn, tk = _pick(m, tm), _pick(n, tn), _pick(kd, tk)
    nk = kd // tk
    return pl.pallas_call(
        functools.partial(_mm_kernel, nk=nk),
        grid=(m // tm, n // tn, nk),
        in_specs=[pl.BlockSpec((tm, tk), lambda i, j, k: (i, k)),
                  pl.BlockSpec((tk, tn), lambda i, j, k: (k, j))],
        out_specs=pl.BlockSpec((tm, tn), lambda i, j, k: (i, j)),
        out_shape=jax.ShapeDtypeStruct((m, n), out_dtype),
        scratch_shapes=[pltpu.VMEM((tm, tn), F32)] if nk > 1 else [],
        compiler_params=_cparams(("parallel", "parallel", "arbitrary")),
        name=name,
    )(x, w)


def _swiglu_kernel(x_ref, w1_ref, w3_ref, o_ref):
    x = x_ref[...]
    a = jnp.dot(x, w1_ref[...], preferred_element_type=F32)
    b = jnp.dot(x, w3_ref[...], preferred_element_type=F32)
    o_ref[...] = (a * jax.nn.sigmoid(a) * b).astype(o_ref.dtype)


def _swiglu_up(hn, w13):
    m, kd = hn.shape
    f = w13.shape[1] // 2
    tm, tn = _pick(m, 1024), _pick(f, 512)
    nj = f // tn
    return pl.pallas_call(
        _swiglu_kernel,
        grid=(m // tm, nj),
        in_specs=[pl.BlockSpec((tm, kd), lambda i, j: (i, 0)),
                  pl.BlockSpec((kd, tn), lambda i, j: (0, j)),
                  pl.BlockSpec((kd, tn), lambda i, j: (0, j + nj))],
        out_specs=pl.BlockSpec((tm, tn), lambda i, j: (i, j)),
        out_shape=jax.ShapeDtypeStruct((m, f), BF16),
        compiler_params=_cparams(("parallel", "parallel")),
        name="swiglu_up",
    )(hn, w13, w13)


def _merge_kernel(o_ref, wa_ref, y_ref, ws_ref, ga_ref, gs_ref, out_ref):
    a = jnp.dot(o_ref[...], wa_ref[...], preferred_element_type=F32)
    s = jnp.dot(y_ref[...], ws_ref[...], preferred_element_type=F32)
    ga = jax.nn.sigmoid(ga_ref[...].astype(F32))
    gs = jax.nn.sigmoid(gs_ref[...].astype(F32))
    out_ref[...] = (ga * a + gs * s).astype(out_ref.dtype)


def _merge(o_attn, wa, y, ws, proj, ga_off, gs_off, d):
    m = o_attn.shape[0]
    tm, tn = _pick(m, 1024), _pick(d, 512)
    ja, js = ga_off // tn, gs_off // tn
    return pl.pallas_call(
        _merge_kernel,
        grid=(m // tm, d // tn),
        in_specs=[pl.BlockSpec((tm, o_attn.shape[1]), lambda i, j: (i, 0)),
                  pl.BlockSpec((wa.shape[0], tn), lambda i, j: (0, j)),
                  pl.BlockSpec((tm, y.shape[1]), lambda i, j: (i, 0)),
                  pl.BlockSpec((ws.shape[0], tn), lambda i, j: (0, j)),
                  pl.BlockSpec((tm, tn), lambda i, j: (i, ja + j)),
                  pl.BlockSpec((tm, tn), lambda i, j: (i, js + j))],
        out_specs=pl.BlockSpec((tm, tn), lambda i, j: (i, j)),
        out_shape=jax.ShapeDtypeStruct((m, d), BF16),
        compiler_params=_cparams(("parallel", "parallel")),
        name="merge",
    )(o_attn, wa, y, ws, proj, proj)


def _rope_tables(pos, invf, sign):
    ang = pos * invf
    return jnp.cos(ang), jnp.sin(ang) * sign


def _rope128(x, cos, sin_signed):
    return x * cos + pltpu.roll(x, HEAD_DIM // 2, 1) * sin_signed


def _rope64(x, cos, sin_signed, first_half):
    half = IDX_DIM // 2
    rot = jnp.where(first_half, pltpu.roll(x, LANES - half, 1), pltpu.roll(x, half, 1))
    return x * cos + rot * sin_signed


def _attn_prep_kernel(pos_ref, q_ref, qi_ref, k_ref, sm_ref, f128_ref, s128_ref, f64_ref, s64_ref,
                      lng_ref, lnb_ref, qs_ref, qis_ref, kr_ref, kid_ref):
    pos = pos_ref[...].astype(F32)
    cos_a, sin_a = _rope_tables(pos, f128_ref[...], s128_ref[...])
    cos_b, sin_b = _rope_tables(pos, f64_ref[...], s64_ref[...])
    lane = lax.broadcasted_iota(jnp.int32, (Q_BLOCK, LANES), 1)
    first_half = (lane & (IDX_DIM - 1)) < IDX_DIM // 2
    low_head = lane < IDX_DIM
    scale = HEAD_DIM ** -0.5
    for h in range(N_HEADS):
        xh = q_ref[:, h * HEAD_DIM:(h + 1) * HEAD_DIM].astype(F32)
        r = _rope128(xh, cos_a, sin_a) * scale
        hk, g = divmod(h, GQA)
        qs_ref[0, hk, g * Q_BLOCK:(g + 1) * Q_BLOCK, :] = r.astype(qs_ref.dtype)
    for h in range(N_KV_HEADS):
        xh = k_ref[:, h * HEAD_DIM:(h + 1) * HEAD_DIM].astype(F32)
        kr_ref[:, h * HEAD_DIM:(h + 1) * HEAD_DIM] = _rope128(xh, cos_a, sin_a).astype(kr_ref.dtype)
    zero = jnp.zeros((Q_BLOCK, LANES), F32)
    for j in range(IDX_HEADS // 2):
        xp = qi_ref[:, j * LANES:(j + 1) * LANES].astype(F32)
        r = _rope64(xp, cos_b, sin_b, first_half)
        qis_ref[0, 2 * j] = jnp.where(low_head, r, zero).astype(qis_ref.dtype)
        qis_ref[0, 2 * j + 1] = jnp.where(low_head, zero, r).astype(qis_ref.dtype)
    ki = sm_ref[:, 0:LANES]
    mu = jnp.mean(ki, axis=-1, keepdims=True)
    var = jnp.mean(jnp.square(ki - mu), axis=-1, keepdims=True)
    kn = (ki - mu) * lax.rsqrt(var + EPS) * lng_ref[...] + lnb_ref[...]
    kid_ref[...] = _rope64(kn, cos_b, sin_b, first_half).astype(kid_ref.dtype)


def _attn_prep(pos, proj, small, offs, idx_ln_g, idx_ln_b):
    m = proj.shape[0]
    nblk = m // Q_BLOCK
    half_a, half_b = HEAD_DIM // 2, IDX_DIM // 2
    inv_a = 1.0 / (ROPE_THETA ** (jnp.arange(half_a, dtype=F32) * (2.0 / HEAD_DIM)))
    inv_b = 1.0 / (ROPE_THETA ** (jnp.arange(half_b, dtype=F32) * (2.0 / IDX_DIM)))
    f128 = jnp.tile(inv_a, 2).reshape(1, LANES)
    f64 = jnp.tile(inv_b, 4).reshape(1, LANES)
    s128 = jnp.asarray(np.repeat([-1.0, 1.0], half_a), F32).reshape(1, LANES)
    s64 = jnp.asarray(np.tile(np.repeat([-1.0, 1.0], half_b), 2), F32).reshape(1, LANES)
    lng = jnp.tile(idx_ln_g, 2).reshape(1, LANES)
    lnb = jnp.tile(idx_ln_b, 2).reshape(1, LANES)
    vec = pl.BlockSpec((1, LANES), lambda i: (0, 0))
    return pl.pallas_call(
        _attn_prep_kernel,
        grid=(nblk,),
        in_specs=[pl.BlockSpec((Q_BLOCK, 1), lambda i: (i, 0)),
                  pl.BlockSpec((Q_BLOCK, D_ATTN), lambda i: (i, offs["q"] // D_ATTN)),
                  pl.BlockSpec((Q_BLOCK, D_IDX), lambda i: (i, offs["qi"] // D_IDX)),
                  pl.BlockSpec((Q_BLOCK, D_KV), lambda i: (i, offs["k"] // D_KV)),
                  pl.BlockSpec((Q_BLOCK, small.shape[1]), lambda i: (i, 0)),
                  vec, vec, vec, vec, vec, vec],
        out_specs=[pl.BlockSpec((1, N_KV_HEADS, GQA * Q_BLOCK, HEAD_DIM), lambda i: (i, 0, 0, 0)),
                   pl.BlockSpec((1, IDX_HEADS, Q_BLOCK, LANES), lambda i: (i, 0, 0, 0)),
                   pl.BlockSpec((Q_BLOCK, D_KV), lambda i: (i, 0)),
                   pl.BlockSpec((Q_BLOCK, LANES), lambda i: (i, 0))],
        out_shape=[jax.ShapeDtypeStruct((nblk, N_KV_HEADS, GQA * Q_BLOCK, HEAD_DIM), BF16),
                   jax.ShapeDtypeStruct((nblk, IDX_HEADS, Q_BLOCK, LANES), BF16),
                   jax.ShapeDtypeStruct((m, D_KV), BF16),
                   jax.ShapeDtypeStruct((m, LANES), BF16)],
        compiler_params=_cparams(("parallel",)),
        name="attn_prep",
    )(pos, proj, proj, proj, small, f128, s128, f64, s64, lng, lnb)


def _dsa_kernel(qs_ref, qis_ref, sm_ref, kid_ref, kr_ref, v_ref, o_ref,
                wb_scr, key_scr, bias_scr, m_scr, l_scr, acc_scr, *, topk, wi_off):
    i = pl.program_id(1)
    nkb = (i * Q_BLOCK + Q_BLOCK + KEY_BLOCK - 1) // KEY_BLOCK
    nsub = KEY_BLOCK // LANES
    wi_scale = IDX_HEADS ** -0.5 * IDX_DIM ** -0.5

    for h in range(IDX_HEADS):
        col = sm_ref[:, wi_off + h:wi_off + h + 1] * wi_scale
        wb_scr[h] = jnp.broadcast_to(col, (Q_BLOCK, LANES))

    row_pos = i * Q_BLOCK + lax.broadcasted_iota(jnp.int32, (Q_BLOCK, KEY_BLOCK), 0)
    col_iota = lax.broadcasted_iota(jnp.int32, (Q_BLOCK, KEY_BLOCK), 1)

    def score_block(kb, carry):
        k0 = pl.multiple_of(kb * KEY_BLOCK, KEY_BLOCK)
        ki = kid_ref[pl.ds(k0, KEY_BLOCK), :]
        score = jnp.zeros((Q_BLOCK, KEY_BLOCK), F32)
        for hg in range(IDX_HEADS // IDX_HEAD_GROUP):
            h0 = hg * IDX_HEAD_GROUP
            q2 = qis_ref[0, h0:h0 + IDX_HEAD_GROUP].reshape(IDX_HEAD_GROUP * Q_BLOCK, LANES)
            d = lax.dot_general(q2, ki, (((1,), (1,)), ((), ())), preferred_element_type=F32)
            for u in range(IDX_HEAD_GROUP):
                w = wb_scr[h0 + u]
                score = score + jnp.maximum(d[u * Q_BLOCK:(u + 1) * Q_BLOCK], 0.0) * jnp.concatenate([w] * nsub, axis=1)
        bits = lax.bitcast_convert_type(score, jnp.int32)
        skey = bits ^ ((bits >> 31) & jnp.int32(0x7FFFFFFF))
        causal = (k0 + col_iota) <= row_pos
        key_scr[kb] = jnp.where(causal, skey, jnp.int32(INT_MIN))
        return carry

    lax.fori_loop(0, nkb, score_block, 0)

    def bit_step(it, thr):
        cand = thr ^ lax.shift_left(jnp.int32(1), 31 - it)
        cand_b = jnp.broadcast_to(cand, (Q_BLOCK, LANES))

        def count_block(kb, acc):
            blk = key_scr[kb]
            for c in range(nsub):
                acc = acc + jnp.where(blk[:, c * LANES:(c + 1) * LANES] >= cand_b, 1, 0)
            return acc

        cnt = lax.fori_loop(0, nkb, count_block, jnp.zeros((Q_BLOCK, LANES), jnp.int32))
        total = jnp.sum(cnt, axis=1, keepdims=True)
        return jnp.where(total >= topk, cand, thr)

    thr = lax.fori_loop(0, 32, bit_step, jnp.full((Q_BLOCK, 1), INT_MIN, jnp.int32))
    thr = jnp.maximum(thr, jnp.int32(INT_MIN + 1))
    thr_b = jnp.broadcast_to(thr, (Q_BLOCK, KEY_BLOCK))

    def bias_block(kb, carry):
        bias_scr[kb] = jnp.where(key_scr[kb] >= thr_b, 0.0, NEG_BIG).astype(F32)
        return carry

    lax.fori_loop(0, nkb, bias_block, 0)

    def masked_logits(kb, hk):
        k0 = pl.multiple_of(kb * KEY_BLOCK, KEY_BLOCK)
        kblk = kr_ref[pl.ds(k0, KEY_BLOCK), hk * HEAD_DIM:(hk + 1) * HEAD_DIM]
        s = lax.dot_general(qs_ref[0, hk], kblk, (((1,), (1,)), ((), ())), preferred_element_type=F32)
        return s + jnp.concatenate([bias_scr[kb]] * GQA, axis=0)

    m_scr[...] = jnp.full(m_scr.shape, NEG_BIG, F32)

    def sweep(body):
        nfull = nkb // ATTN_UNROLL

        def trip(t, carry):
            for u in range(ATTN_UNROLL):
                body(t * ATTN_UNROLL + u)
            return carry

        def single(kb, carry):
            body(kb)
            return carry

        lax.fori_loop(0, nfull, trip, 0)
        lax.fori_loop(nfull * ATTN_UNROLL, nkb, single, 0)

    def max_block(kb):
        for hk in range(N_KV_HEADS):
            s = masked_logits(kb, hk)
            part = s[:, 0:LANES]
            for c in range(1, nsub):
                part = jnp.maximum(part, s[:, c * LANES:(c + 1) * LANES])
            m_scr[hk] = jnp.maximum(m_scr[hk], part)

    sweep(max_block)
    for hk in range(N_KV_HEADS):
        row_max = jnp.max(m_scr[hk], axis=1, keepdims=True)
        m_scr[hk] = jnp.broadcast_to(row_max, m_scr.shape[1:])
    l_scr[...] = jnp.zeros(l_scr.shape, F32)
    acc_scr[...] = jnp.zeros(acc_scr.shape, F32)

    def pv_block(kb):
        k0 = pl.multiple_of(kb * KEY_BLOCK, KEY_BLOCK)
        for hk in range(N_KV_HEADS):
            vblk = v_ref[pl.ds(k0, KEY_BLOCK), hk * HEAD_DIM:(hk + 1) * HEAD_DIM]
            p = jnp.exp(masked_logits(kb, hk) - jnp.concatenate([m_scr[hk]] * nsub, axis=1))
            part = p[:, 0:LANES]
            for c in range(1, nsub):
                part = part + p[:, c * LANES:(c + 1) * LANES]
            l_scr[hk] += part
            acc_scr[hk] += jnp.dot(p.astype(BF16), vblk, preferred_element_type=F32)

    sweep(pv_block)
    for hk in range(N_KV_HEADS):
        out = acc_scr[hk] / jnp.sum(l_scr[hk], axis=1, keepdims=True)
        for g in range(GQA):
            h = hk * GQA + g
            o_ref[:, h * HEAD_DIM:(h + 1) * HEAD_DIM] = out[g * Q_BLOCK:(g + 1) * Q_BLOCK].astype(o_ref.dtype)


def _dsa(qs, qis, small, kid, kr, proj, offs, bsz, s_len, wi_off):
    nblk = s_len // Q_BLOCK
    nkb = s_len // KEY_BLOCK
    topk = min(MAX_TOPK, s_len // 4)
    vcol = offs["v"] // D_KV
    return pl.pallas_call(
        functools.partial(_dsa_kernel, topk=topk, wi_off=wi_off),
        grid=(bsz, nblk),
        in_specs=[pl.BlockSpec((1, N_KV_HEADS, GQA * Q_BLOCK, HEAD_DIM), lambda b, i: (b * nblk + i, 0, 0, 0)),
                  pl.BlockSpec((1, IDX_HEADS, Q_BLOCK, LANES), lambda b, i: (b * nblk + i, 0, 0, 0)),
                  pl.BlockSpec((Q_BLOCK, small.shape[1]), lambda b, i: (b * nblk + i, 0)),
                  pl.BlockSpec((s_len, LANES), lambda b, i: (b, 0)),
                  pl.BlockSpec((s_len, D_KV), lambda b, i: (b, 0)),
                  pl.BlockSpec((s_len, D_KV), lambda b, i: (b, vcol))],
        out_specs=pl.BlockSpec((Q_BLOCK, D_ATTN), lambda b, i: (b * nblk + i, 0)),
        out_shape=jax.ShapeDtypeStruct((bsz * s_len, D_ATTN), BF16),
        scratch_shapes=[pltpu.VMEM((IDX_HEADS, Q_BLOCK, LANES), F32),
                        pltpu.VMEM((nkb, Q_BLOCK, KEY_BLOCK), jnp.int32),
                        pltpu.VMEM((nkb, Q_BLOCK, KEY_BLOCK), F32),
                        pltpu.VMEM((N_KV_HEADS, GQA * Q_BLOCK, LANES), F32),
                        pltpu.VMEM((N_KV_HEADS, GQA * Q_BLOCK, LANES), F32),
                        pltpu.VMEM((N_KV_HEADS, GQA * Q_BLOCK, HEAD_DIM), F32)],
        compiler_params=_cparams(("parallel", "arbitrary")),
        name="dsa",
    )(qs, qis, small, kid, kr, proj)


def _conv_kernel(x_ref, halo_ref, w_ref, b_ref, o_ref, *, tiles_per_seq):
    st = x_ref.shape[0]
    first = (pl.program_id(0) % tiles_per_seq) == 0
    halo = jnp.where(first, 0.0, halo_ref[...].astype(F32))
    xx = jnp.concatenate([halo, x_ref[...].astype(F32)], axis=0)
    acc = b_ref[...] + w_ref[CONV_WIDTH - 1:CONV_WIDTH, :] * xx[SUBLANES:, :]
    for k in range(CONV_WIDTH - 1):
        shifted = pltpu.roll(xx, CONV_WIDTH - 1 - k, 0)[SUBLANES:, :]
        acc = acc + w_ref[k:k + 1, :] * shifted
    o_ref[...] = (acc * jax.nn.sigmoid(acc)).astype(o_ref.dtype)


def _conv(proj, xs_off, conv_w, conv_b, s_len):
    m = proj.shape[0]
    ch = conv_w.shape[1]
    st, ct = _pick(s_len, 512), _pick(ch, 512)
    c0 = xs_off // ct
    hb = st // SUBLANES
    return pl.pallas_call(
        functools.partial(_conv_kernel, tiles_per_seq=s_len // st),
        grid=(m // st, ch // ct),
        in_specs=[pl.BlockSpec((st, ct), lambda r, c: (r, c0 + c)),
                  pl.BlockSpec((SUBLANES, ct), lambda r, c: (jnp.maximum(r * hb - 1, 0), c0 + c)),
                  pl.BlockSpec((CONV_WIDTH, ct), lambda r, c: (0, c)),
                  pl.BlockSpec((1, ct), lambda r, c: (0, c))],
        out_specs=pl.BlockSpec((st, ct), lambda r, c: (r, c)),
        out_shape=jax.ShapeDtypeStruct((m, ch), BF16),
        compiler_params=_cparams(("parallel", "parallel")),
        name="conv",
    )(proj, proj, conv_w, conv_b.reshape(1, ch))


def _softplus(x):
    return jnp.maximum(x, 0.0) + jnp.log1p(jnp.exp(-jnp.abs(x)))


def _split_bf16(v, parts):
    out = []
    for _ in range(parts):
        p = v.astype(BF16)
        out.append(p)
        v = v - p.astype(F32)
    return out


def _ssd_kernel(x_ref, b_ref, c_ref, z_ref, dtc_ref, pc_ref, dsk_ref, ng_ref, o_ref, state_scr, *, heads):
    gw = heads * SSD_HEAD_DIM
    rep = LANES // heads
    cl = SSD_CHUNK

    @pl.when(pl.program_id(2) == 0)
    def _():
        state_scr[...] = jnp.zeros(state_scr.shape, F32)

    li = lax.broadcasted_iota(jnp.int32, (cl, cl), 0)
    si = lax.broadcasted_iota(jnp.int32, (cl, cl), 1)
    tri = li >= si
    tri_b = tri.astype(BF16)
    er = lax.broadcasted_iota(jnp.int32, (LANES, gw), 0)
    ec = lax.broadcasted_iota(jnp.int32, (LANES, gw), 1)
    expand = (er == lax.shift_right_logical(ec, SSD_HEAD_DIM.bit_length() - 1) * rep).astype(BF16)
    lane = lax.broadcasted_iota(jnp.int32, (cl, LANES), 1)
    neg_a = -jnp.exp(pc_ref[0, 1:2, :])

    state = state_scr[...]
    for sc in range(x_ref.shape[0] // cl):
        rows = slice(sc * cl, (sc + 1) * cl)
        dtc = _softplus(dtc_ref[0, 0, rows, :] + pc_ref[0, 0:1, :])
        adt_c = dtc * neg_a
        acs_c = sum(jnp.dot(tri_b, p, preferred_element_type=F32) for p in _split_bf16(adt_c, 3))
        acs_r = acs_c.T
        last = acs_c[cl - 1:cl, :]
        dec_hi, dec_lo = _split_bf16(jnp.exp(acs_c), 2)
        stacked = jnp.concatenate([dtc.astype(BF16), (dtc * jnp.exp(last - acs_c)).astype(BF16), dec_hi, dec_lo],
                                  axis=0)
        ex = jnp.dot(stacked, expand, preferred_element_type=F32)
        dt_e, dts_e, dec_e = ex[0:cl], ex[cl:2 * cl], ex[2 * cl:3 * cl] + ex[3 * cl:4 * cl]

        xg = x_ref[rows, :].astype(F32)
        xdt = (xg * dt_e).astype(BF16)
        xst = (xg * dts_e).astype(BF16)
        bm = b_ref[rows, :]
        cm = c_ref[rows, :]
        cb = lax.dot_general(cm, bm, (((1,), (1,)), ((), ())), preferred_element_type=F32)
        pieces = []
        for j in range(gw // LANES):
            ys = []
            for r in (2 * j, 2 * j + 1):
                col = jnp.broadcast_to(acs_c[:, r * rep:r * rep + 1], (cl, cl))
                row = jnp.broadcast_to(acs_r[r * rep:r * rep + 1, :], (cl, cl))
                lmat = jnp.where(tri, jnp.exp(col - row), 0.0)
                ys.append(jnp.dot((cb * lmat).astype(BF16), xdt[:, j * LANES:(j + 1) * LANES],
                                  preferred_element_type=F32))
            pieces.append(jnp.where(lane < SSD_HEAD_DIM, ys[0], ys[1]))
        y_diag = jnp.concatenate(pieces, axis=1) if len(pieces) > 1 else pieces[0]

        y_off = jnp.dot(cm, state.astype(BF16), preferred_element_type=F32) * dec_e
        y = y_diag + y_off + xg * dsk_ref[0]
        state = state * dec_e[cl - 1:cl, :] + lax.dot_general(
            bm, xst, (((0,), (0,)), ((), ())), preferred_element_type=F32)

        z = z_ref[rows, :].astype(F32)
        yz = y * (z * jax.nn.sigmoid(z))
        o_ref[rows, :] = _rms(yz, ng_ref[...]).astype(o_ref.dtype)
    state_scr[...] = state


def _ssd(conv_out, proj, small, z_off, dt_off, dt_bias, a_log, d_skip, norm_g, bsz, s_len):
    d_ssd = norm_g.shape[0]
    n_heads = a_log.shape[0]
    heads = n_heads // SSD_GROUPS
    gw = heads * SSD_HEAD_DIM
    rep = LANES // heads
    m = bsz * s_len
    dt = small[:, dt_off:dt_off + n_heads].reshape(bsz, s_len, SSD_GROUPS, heads)
    dtc = jnp.transpose(jnp.repeat(dt, rep, axis=-1), (0, 2, 1, 3))
    par = jnp.repeat(jnp.stack([dt_bias, a_log]).reshape(2, SSD_GROUPS, heads), rep, axis=-1)
    pc = jnp.transpose(par, (1, 0, 2))
    dsk = jnp.repeat(d_skip, SSD_HEAD_DIM).reshape(SSD_GROUPS, 1, gw)
    b0 = d_ssd // SSD_STATE
    c0 = b0 + SSD_GROUPS
    z0 = z_off // gw
    rows = SSD_STEP_CHUNKS * SSD_CHUNK
    assert s_len % rows == 0
    nc = s_len // rows
    return pl.pallas_call(
        functools.partial(_ssd_kernel, heads=heads),
        grid=(bsz, SSD_GROUPS, nc),
        in_specs=[pl.BlockSpec((rows, gw), lambda b, g, c: (b * nc + c, g)),
                  pl.BlockSpec((rows, SSD_STATE), lambda b, g, c: (b * nc + c, b0 + g)),
                  pl.BlockSpec((rows, SSD_STATE), lambda b, g, c: (b * nc + c, c0 + g)),
                  pl.BlockSpec((rows, gw), lambda b, g, c: (b * nc + c, z0 + g)),
                  pl.BlockSpec((1, 1, rows, LANES), lambda b, g, c: (b, g, c, 0)),
                  pl.BlockSpec((1, 2, LANES), lambda b, g, c: (g, 0, 0)),
                  pl.BlockSpec((1, 1, gw), lambda b, g, c: (g, 0, 0)),
                  pl.BlockSpec((1, gw), lambda b, g, c: (0, g))],
        out_specs=pl.BlockSpec((rows, gw), lambda b, g, c: (b * nc + c, g)),
        out_shape=jax.ShapeDtypeStruct((m, d_ssd), BF16),
        scratch_shapes=[pltpu.VMEM((SSD_STATE, gw), F32)],
        compiler_params=_cparams(("parallel", "parallel", "arbitrary")),
        name="ssd",
    )(conv_out, conv_out, conv_out, proj, dtc, pc, dsk, norm_g.reshape(1, d_ssd))


def _ffn(hn, x, w13_stack, w2_stack, layer, g_post, g_next):
    h = _swiglu_up(hn, _layer_bf16(w13_stack, layer))
    mm = _matmul(h, _layer_bf16(w2_stack, layer), F32, name="ffn_down")
    return _post(mm, x, g_post, g_next, 0.5)


def _mixer(hn, x, pos, w_in_bf16_stack, idx_ln_g, idx_ln_b, conv_w, conv_b, dt_bias, a_log, d_skip, ssd_norm_g,
           wa_stack, ws_stack, wo_stack, layer, g_post, g_next, bsz, s_len):
    d = x.shape[1]
    d_ssd = ssd_norm_g.shape[0]
    n_ssd_heads = a_log.shape[0]
    gn = SSD_GROUPS * SSD_STATE
    sizes = (D_ATTN, D_KV, D_KV, D_IDX, IDX_DIM, IDX_HEADS, d_ssd, d_ssd, gn, gn, n_ssd_heads, d, d)
    names = ("q", "k", "v", "qi", "ki", "wi", "z", "xs", "bs", "cs", "dt", "ga", "gs")
    starts = {nm: int(s0) for nm, s0 in zip(names, np.concatenate([[0], np.cumsum(sizes)[:-1]]))}
    size = dict(zip(names, sizes))
    order = ("q", "qi", "k", "v", "z", "xs", "bs", "cs", "ga", "gs")
    offs, o = {}, 0
    for nm in order:
        offs[nm] = o
        o += size[nm]
    dt_off = 2 * IDX_DIM
    wi_off = dt_off + n_ssd_heads
    used = wi_off + IDX_HEADS
    n_small = -(-used // LANES) * LANES
    small_segments = ((starts["ki"], IDX_DIM, 0), (starts["ki"], IDX_DIM, IDX_DIM),
                      (starts["dt"], n_ssd_heads, dt_off), (starts["wi"], IDX_HEADS, wi_off))
    w_big, w_small = _w_in_relayout(w_in_bf16_stack, layer,
                                    tuple((starts[nm], size[nm], offs[nm]) for nm in order), o,
                                    small_segments, n_small)

    proj = _matmul(hn, w_big, BF16, name="in_proj")
    small = _matmul(hn, w_small, F32, name="in_proj_small")

    qs, qis, kr, kid = _attn_prep(pos, proj, small, offs, idx_ln_g, idx_ln_b)
    o_attn = _dsa(qs, qis, small, kid, kr, proj, offs, bsz, s_len, wi_off)

    conv_out = _conv(proj, offs["xs"], conv_w, conv_b, s_len)
    y = _ssd(conv_out, proj, small, offs["z"], dt_off, dt_bias, a_log, d_skip, ssd_norm_g, bsz, s_len)

    merged = _merge(o_attn, _layer_bf16(wa_stack, layer), y, _layer_bf16(ws_stack, layer), proj,
                    offs["ga"], offs["gs"], d)
    mm = _matmul(merged, _layer_bf16(wo_stack, layer), F32, name="out_proj")
    return _post(mm, x, g_post, g_next, 1.0)


def kernel(x, positions, ffn1_pre_g, ffn1_w13, ffn1_w2, ffn1_post_g, mix_pre_g, w_in, idx_ln_g, idx_ln_b, conv_w, conv_b, dt_bias, a_log, d_skip, ssd_norm_g, w_attn_o, w_ssd_o, w_out, mix_post_g, ffn2_pre_g, ffn2_w13, ffn2_w2, ffn2_post_g):
    bsz, s_len, d = x.shape
    depth = ffn1_w13.shape[0]
    assert s_len % KEY_BLOCK == 0 and s_len % SSD_CHUNK == 0 and d % LANES == 0
    m = bsz * s_len
    xf = x.reshape(m, d)
    pos = positions.reshape(m, 1)
    hn = _prenorm(xf, ffn1_pre_g[0])
    w_in = w_in.astype(BF16)
    for l in range(depth):
        xf, hn = _ffn(hn, xf, ffn1_w13, ffn1_w2, l, ffn1_post_g[l], mix_pre_g[l])
        xf, hn = _mixer(hn, xf, pos, w_in, idx_ln_g[l], idx_ln_b[l], conv_w[l], conv_b[l], dt_bias[l],
                        a_log[l], d_skip[l], ssd_norm_g[l], w_attn_o, w_ssd_o, w_out, l,
                        mix_post_g[l], ffn2_pre_g[l], bsz, s_len)
        g_next = ffn1_pre_g[l + 1] if l + 1 < depth else None
        xf, hn = _ffn(hn, xf, ffn2_w13, ffn2_w2, l, ffn2_post_g[l], g_next)
    return xf.reshape(bsz, s_len, d)
```

```python
import functools

import jax
import jax.numpy as jnp
import numpy as np
from jax import lax
from jax.experimental import pallas as pl
from jax.experimental.pallas import tpu as pltpu

N_HEADS = 16
N_KV_HEADS = 4
HEAD_DIM = 128
GQA = N_HEADS // N_KV_HEADS
D_ATTN = N_HEADS * HEAD_DIM
D_KV = N_KV_HEADS * HEAD_DIM
IDX_HEADS = 32
IDX_DIM = 64
D_IDX = IDX_HEADS * IDX_DIM
MAX_TOPK = 256
Q_BLOCK = 128
ROPE_THETA = 10000.0
SSD_HEAD_DIM = 64
SSD_GROUPS = 8
SSD_STATE = 128
CONV_WIDTH = 4
SSD_CHUNK = 128
EPS = 1e-6

LANES = 128
SUBLANES = 8
VMEM_LIMIT = 56 * 1024 * 1024

KEY_BLOCK = 256
IDX_HEAD_GROUP = 4
ATTN_UNROLL = 4
SCORE_UNROLL = 2
SSD_STEP_CHUNKS = 4
NEG_BIG = -1e30
INT_MIN = -(2 ** 31)

BF16 = jnp.bfloat16
F32 = jnp.float32


def _cparams(sem):
    return pltpu.CompilerParams(dimension_semantics=sem, vmem_limit_bytes=VMEM_LIMIT)


def _pick(n, pref):
    t = min(pref, n)
    while n % t:
        t //= 2
    return t


def _rms(v, g):
    return v * lax.rsqrt(jnp.mean(v * v, axis=-1, keepdims=True) + EPS) * g


def _prenorm_kernel(x_ref, g_ref, o_ref):
    o_ref[...] = _rms(x_ref[...], g_ref[...]).astype(o_ref.dtype)


def _prenorm(x, g):
    m, d = x.shape
    tr = _pick(m, 256)
    return pl.pallas_call(
        _prenorm_kernel,
        grid=(m // tr,),
        in_specs=[pl.BlockSpec((tr, d), lambda i: (i, 0)), pl.BlockSpec((1, d), lambda i: (0, 0))],
        out_specs=pl.BlockSpec((tr, d), lambda i: (i, 0)),
        out_shape=jax.ShapeDtypeStruct((m, d), BF16),
        compiler_params=_cparams(("parallel",)),
        name="prenorm",
    )(x, g.reshape(1, d))


def _post_kernel(m_ref, x_ref, gp_ref, gn_ref, xo_ref, hn_ref, *, coef):
    xn = x_ref[...] + coef * _rms(m_ref[...], gp_ref[...])
    xo_ref[...] = xn
    hn_ref[...] = _rms(xn, gn_ref[...]).astype(hn_ref.dtype)


def _post_last_kernel(m_ref, x_ref, gp_ref, xo_ref, *, coef):
    xo_ref[...] = x_ref[...] + coef * _rms(m_ref[...], gp_ref[...])


def _post(mm, x, g_post, g_next, coef):
    m, d = x.shape
    tr = _pick(m, 256)
    row = pl.BlockSpec((tr, d), lambda i: (i, 0))
    vec = pl.BlockSpec((1, d), lambda i: (0, 0))
    if g_next is None:
        return pl.pallas_call(
            functools.partial(_post_last_kernel, coef=coef),
            grid=(m // tr,),
            in_specs=[row, row, vec],
            out_specs=row,
            out_shape=jax.ShapeDtypeStruct((m, d), F32),
            compiler_params=_cparams(("parallel",)),
            name="post_last",
        )(mm, x, g_post.reshape(1, d)), None
    return pl.pallas_call(
        functools.partial(_post_kernel, coef=coef),
        grid=(m // tr,),
        in_specs=[row, row, vec, vec],
        out_specs=[row, row],
        out_shape=[jax.ShapeDtypeStruct((m, d), F32), jax.ShapeDtypeStruct((m, d), BF16)],
        compiler_params=_cparams(("parallel",)),
        name="post",
    )(mm, x, g_post.reshape(1, d), g_next.reshape(1, d))


def _cast_kernel(w_ref, o_ref):
    o_ref[...] = w_ref[...].astype(o_ref.dtype)


def _layer_bf16(w_stack, layer):
    _, r, c = w_stack.shape
    tr, tc = _pick(r, 1024), _pick(c, 2048)
    return pl.pallas_call(
        _cast_kernel,
        grid=(r // tr, c // tc),
        in_specs=[pl.BlockSpec((None, tr, tc), lambda i, j: (layer, i, j))],
        out_specs=pl.BlockSpec((tr, tc), lambda i, j: (i, j)),
        out_shape=jax.ShapeDtypeStruct((r, c), BF16),
        compiler_params=_cparams(("parallel", "parallel")),
        name="cast_w",
    )(w_stack)


def _w_in_kernel(w_ref, big_ref, small_ref, *, big_segments, small_segments):
    w = w_ref[...]
    for src, size, dst in big_segments:
        big_ref[:, dst:dst + size] = w[:, src:src + size]
    end = 0
    for src, size, dst in small_segments:
        small_ref[:, dst:dst + size] = w[:, src:src + size]
        end = max(end, dst + size)
    if end < small_ref.shape[1]:
        small_ref[:, end:] = jnp.zeros((small_ref.shape[0], small_ref.shape[1] - end), small_ref.dtype)


def _w_in_relayout(w_stack, layer, big_segments, n_big, small_segments, n_small):
    _, r, c = w_stack.shape
    tr = _pick(r, 128)
    return pl.pallas_call(
        functools.partial(_w_in_kernel, big_segments=big_segments, small_segments=small_segments),
        grid=(r // tr,),
        in_specs=[pl.BlockSpec((None, tr, c), lambda i: (layer, i, 0))],
        out_specs=[pl.BlockSpec((tr, n_big), lambda i: (i, 0)), pl.BlockSpec((tr, n_small), lambda i: (i, 0))],
        out_shape=[jax.ShapeDtypeStruct((r, n_big), w_stack.dtype), jax.ShapeDtypeStruct((r, n_small), w_stack.dtype)],
        compiler_params=_cparams(("parallel",)),
        name="relayout_w_in",
    )(w_stack)


def _mm_kernel(x_ref, w_ref, o_ref, *scratch, nk):
    part = jnp.dot(x_ref[...], w_ref[...], preferred_element_type=F32)
    if nk == 1:
        o_ref[...] = part.astype(o_ref.dtype)
        return
    acc_ref, = scratch
    k = pl.program_id(2)

    @pl.when(k == 0)
    def _():
        acc_ref[...] = part

    @pl.when(k > 0)
    def _():
        acc_ref[...] += part

    @pl.when(k == nk - 1)
    def _():
        o_ref[...] = acc_ref[...].astype(o_ref.dtype)


def _matmul(x, w, out_dtype, tm=1024, tn=1024, tk=4096, name="matmul"):
    m, kd = x.shape
    n = w.shape[1]
    tm, tn, tk = _pick(m, tm), _pick(n, tn), _pick(kd, tk)
    nk = kd // tk
    return pl.pallas_call(
        functools.partial(_mm_kernel, nk=nk),
        grid=(m // tm, n // tn, nk),
        in_specs=[pl.BlockSpec((tm, tk), lambda i, j, k: (i, k)),
                  pl.BlockSpec((tk, tn), lambda i, j, k: (k, j))],
        out_specs=pl.BlockSpec((tm, tn), lambda i, j, k: (i, j)),
        out_shape=jax.ShapeDtypeStruct((m, n), out_dtype),
        scratch_shapes=[pltpu.VMEM((tm, tn), F32)] if nk > 1 else [],
        compiler_params=_cparams(("parallel", "parallel", "arbitrary")),
        name=name,
    )(x, w)


def _swiglu_kernel(x_ref, w1_ref, w3_ref, o_ref):
    x = x_ref[...]
    a = jnp.dot(x, w1_ref[...], preferred_element_type=F32)
    b = jnp.dot(x, w3_ref[...], preferred_element_type=F32)
    o_ref[...] = (a * jax.nn.sigmoid(a) * b).astype(o_ref.dtype)


def _swiglu_up(hn, w13):
    m, kd = hn.shape
    f = w13.shape[1] // 2
    tm, tn = _pick(m, 1024), _pick(f, 512)
    nj = f // tn
    return pl.pallas_call(
        _swiglu_kernel,
        grid=(m // tm, nj),
        in_specs=[pl.BlockSpec((tm, kd), lambda i, j: (i, 0)),
                  pl.BlockSpec((kd, tn), lambda i, j: (0, j)),
                  pl.BlockSpec((kd, tn), lambda i, j: (0, j + nj))],
        out_specs=pl.BlockSpec((tm, tn), lambda i, j: (i, j)),
        out_shape=jax.ShapeDtypeStruct((m, f), BF16),
        compiler_params=_cparams(("parallel", "parallel")),
        name="swiglu_up",
    )(hn, w13, w13)


def _merge_kernel(o_ref, wa_ref, y_ref, ws_ref, ga_ref, gs_ref, out_ref):
    a = jnp.dot(o_ref[...], wa_ref[...], preferred_element_type=F32)
    s = jnp.dot(y_ref[...], ws_ref[...], preferred_element_type=F32)
    ga = jax.nn.sigmoid(ga_ref[...].astype(F32))
    gs = jax.nn.sigmoid(gs_ref[...].astype(F32))
    out_ref[...] = (ga * a + gs * s).astype(out_ref.dtype)


def _merge(o_attn, wa, y, ws, proj, ga_off, gs_off, d):
    m = o_attn.shape[0]
    tm, tn = _pick(m, 1024), _pick(d, 512)
    ja, js = ga_off // tn, gs_off // tn
    return pl.pallas_call(
        _merge_kernel,
        grid=(m // tm, d // tn),
        in_specs=[pl.BlockSpec((tm, o_attn.shape[1]), lambda i, j: (i, 0)),
                  pl.BlockSpec((wa.shape[0], tn), lambda i, j: (0, j)),
                  pl.BlockSpec((tm, y.shape[1]), lambda i, j: (i, 0)),
                  pl.BlockSpec((ws.shape[0], tn), lambda i, j: (0, j)),
                  pl.BlockSpec((tm, tn), lambda i, j: (i, ja + j)),
                  pl.BlockSpec((tm, tn), lambda i, j: (i, js + j))],
        out_specs=pl.BlockSpec((tm, tn), lambda i, j: (i, j)),
        out_shape=jax.ShapeDtypeStruct((m, d), BF16),
        compiler_params=_cparams(("parallel", "parallel")),
        name="merge",
    )(o_attn, wa, y, ws, proj, proj)


def _rope_tables(pos, invf, sign):
    ang = pos * invf
    return jnp.cos(ang), jnp.sin(ang) * sign


def _rope128(x, cos, sin_signed):
    return x * cos + pltpu.roll(x, HEAD_DIM // 2, 1) * sin_signed


def _rope64(x, cos, sin_signed, first_half):
    half = IDX_DIM // 2
    rot = jnp.where(first_half, pltpu.roll(x, LANES - half, 1), pltpu.roll(x, half, 1))
    return x * cos + rot * sin_signed


def _attn_prep_kernel(pos_ref, q_ref, qi_ref, k_ref, sm_ref, f128_ref, s128_ref, f64_ref, s64_ref,
                      lng_ref, lnb_ref, qs_ref, qis_ref, kr_ref, kid_ref):
    pos = pos_ref[...].astype(F32)
    cos_a, sin_a = _rope_tables(pos, f128_ref[...], s128_ref[...])
    cos_b, sin_b = _rope_tables(pos, f64_ref[...], s64_ref[...])
    lane = lax.broadcasted_iota(jnp.int32, (Q_BLOCK, LANES), 1)
    first_half = (lane & (IDX_DIM - 1)) < IDX_DIM // 2
    low_head = lane < IDX_DIM
    scale = HEAD_DIM ** -0.5
    for h in range(N_HEADS):
        xh = q_ref[:, h * HEAD_DIM:(h + 1) * HEAD_DIM].astype(F32)
        r = _rope128(xh, cos_a, sin_a) * scale
        hk, g = divmod(h, GQA)
        qs_ref[0, hk, g * Q_BLOCK:(g + 1) * Q_BLOCK, :] = r.astype(qs_ref.dtype)
    for h in range(N_KV_HEADS):
        xh = k_ref[:, h * HEAD_DIM:(h + 1) * HEAD_DIM].astype(F32)
        kr_ref[:, h * HEAD_DIM:(h + 1) * HEAD_DIM] = _rope128(xh, cos_a, sin_a).astype(kr_ref.dtype)
    zero = jnp.zeros((Q_BLOCK, LANES), F32)
    for j in range(IDX_HEADS // 2):
        xp = qi_ref[:, j * LANES:(j + 1) * LANES].astype(F32)
        r = _rope64(xp, cos_b, sin_b, first_half)
        qis_ref[0, 2 * j] = jnp.where(low_head, r, zero).astype(qis_ref.dtype)
        qis_ref[0, 2 * j + 1] = jnp.where(low_head, zero, r).astype(qis_ref.dtype)
    ki = sm_ref[:, 0:LANES]
    mu = jnp.mean(ki, axis=-1, keepdims=True)
    var = jnp.mean(jnp.square(ki - mu), axis=-1, keepdims=True)
    kn = (ki - mu) * lax.rsqrt(var + EPS) * lng_ref[...] + lnb_ref[...]
    kid_ref[...] = _rope64(kn, cos_b, sin_b, first_half).astype(kid_ref.dtype)


def _attn_prep(pos, proj, small, offs, idx_ln_g, idx_ln_b):
    m = proj.shape[0]
    nblk = m // Q_BLOCK
    half_a, half_b = HEAD_DIM // 2, IDX_DIM // 2
    inv_a = 1.0 / (ROPE_THETA ** (jnp.arange(half_a, dtype=F32) * (2.0 / HEAD_DIM)))
    inv_b = 1.0 / (ROPE_THETA ** (jnp.arange(half_b, dtype=F32) * (2.0 / IDX_DIM)))
    f128 = jnp.tile(inv_a, 2).reshape(1, LANES)
    f64 = jnp.tile(inv_b, 4).reshape(1, LANES)
    s128 = jnp.asarray(np.repeat([-1.0, 1.0], half_a), F32).reshape(1, LANES)
    s64 = jnp.asarray(np.tile(np.repeat([-1.0, 1.0], half_b), 2), F32).reshape(1, LANES)
    lng = jnp.tile(idx_ln_g, 2).reshape(1, LANES)
    lnb = jnp.tile(idx_ln_b, 2).reshape(1, LANES)
    vec = pl.BlockSpec((1, LANES), lambda i: (0, 0))
    return pl.pallas_call(
        _attn_prep_kernel,
        grid=(nblk,),
        in_specs=[pl.BlockSpec((Q_BLOCK, 1), lambda i: (i, 0)),
                  pl.BlockSpec((Q_BLOCK, D_ATTN), lambda i: (i, offs["q"] // D_ATTN)),
                  pl.BlockSpec((Q_BLOCK, D_IDX), lambda i: (i, offs["qi"] // D_IDX)),
                  pl.BlockSpec((Q_BLOCK, D_KV), lambda i: (i, offs["k"] // D_KV)),
                  pl.BlockSpec((Q_BLOCK, small.shape[1]), lambda i: (i, 0)),
                  vec, vec, vec, vec, vec, vec],
        out_specs=[pl.BlockSpec((1, N_KV_HEADS, GQA * Q_BLOCK, HEAD_DIM), lambda i: (i, 0, 0, 0)),
                   pl.BlockSpec((1, IDX_HEADS, Q_BLOCK, LANES), lambda i: (i, 0, 0, 0)),
                   pl.BlockSpec((Q_BLOCK, D_KV), lambda i: (i, 0)),
                   pl.BlockSpec((Q_BLOCK, LANES), lambda i: (i, 0))],
        out_shape=[jax.ShapeDtypeStruct((nblk, N_KV_HEADS, GQA * Q_BLOCK, HEAD_DIM), BF16),
                   jax.ShapeDtypeStruct((nblk, IDX_HEADS, Q_BLOCK, LANES), BF16),
                   jax.ShapeDtypeStruct((m, D_KV), BF16),
                   jax.ShapeDtypeStruct((m, LANES), BF16)],
        compiler_params=_cparams(("parallel",)),
        name="attn_prep",
    )(pos, proj, proj, proj, small, f128, s128, f64, s64, lng, lnb)


def _dsa_kernel(qs_ref, qis_ref, sm_ref, kid_ref, kr_ref, v_ref, o_ref,
                wb_scr, key_scr, bias_scr, m_scr, l_scr, acc_scr, *, topk, wi_off):
    i = pl.program_id(1)
    nkb = (i * Q_BLOCK + Q_BLOCK + KEY_BLOCK - 1) // KEY_BLOCK
    nsub = KEY_BLOCK // LANES
    wi_scale = IDX_HEADS ** -0.5 * IDX_DIM ** -0.5

    for h in range(IDX_HEADS):
        col = sm_ref[:, wi_off + h:wi_off + h + 1] * wi_scale
        wb_scr[h] = jnp.broadcast_to(col, (Q_BLOCK, LANES))

    row_pos = i * Q_BLOCK + lax.broadcasted_iota(jnp.int32, (Q_BLOCK, KEY_BLOCK), 0)
    col_iota = lax.broadcasted_iota(jnp.int32, (Q_BLOCK, KEY_BLOCK), 1)

    def sweep(body, unroll):
        start = 0
        while unroll >= 1:
            trips = (nkb - start) // unroll

            def trip(t, carry, start=start, unroll=unroll):
                for u in range(unroll):
                    body(start + t * unroll + u)
                return carry

            lax.fori_loop(0, trips, trip, 0)
            start = start + trips * unroll
            unroll //= 2

    def score_block(kb):
        k0 = pl.multiple_of(kb * KEY_BLOCK, KEY_BLOCK)
        ki = kid_ref[pl.ds(k0, KEY_BLOCK), :]
        score = jnp.zeros((Q_BLOCK, KEY_BLOCK), F32)
        for hg in range(IDX_HEADS // IDX_HEAD_GROUP):
            h0 = hg * IDX_HEAD_GROUP
            q2 = qis_ref[0, h0:h0 + IDX_HEAD_GROUP].reshape(IDX_HEAD_GROUP * Q_BLOCK, LANES)
            d = lax.dot_general(q2, ki, (((1,), (1,)), ((), ())), preferred_element_type=F32)
            for u in range(IDX_HEAD_GROUP):
                w = wb_scr[h0 + u]
                score = score + jnp.maximum(d[u * Q_BLOCK:(u + 1) * Q_BLOCK], 0.0) * jnp.concatenate([w] * nsub, axis=1)
        bits = lax.bitcast_convert_type(score, jnp.int32)
        skey = bits ^ ((bits >> 31) & jnp.int32(0x7FFFFFFF))
        causal = (k0 + col_iota) <= row_pos
        key_scr[kb] = jnp.where(causal, skey, jnp.int32(INT_MIN))

    sweep(score_block, SCORE_UNROLL)

    def bit_step(it, thr):
        cand = thr ^ lax.shift_left(jnp.int32(1), 31 - it)
        cand_b = jnp.broadcast_to(cand, (Q_BLOCK, LANES))

        def count_block(kb, acc):
            blk = key_scr[kb]
            for c in range(nsub):
                acc = acc + jnp.where(blk[:, c * LANES:(c + 1) * LANES] >= cand_b, 1, 0)
            return acc

        cnt = lax.fori_loop(0, nkb, count_block, jnp.zeros((Q_BLOCK, LANES), jnp.int32))
        total = jnp.sum(cnt.astype(F32), axis=1, keepdims=True)
        return jnp.where(total >= float(topk), cand, thr)

    thr = lax.fori_loop(0, 32, bit_step, jnp.full((Q_BLOCK, 1), INT_MIN, jnp.int32))
    thr = jnp.maximum(thr, jnp.int32(INT_MIN + 1))
    thr_b = jnp.broadcast_to(thr, (Q_BLOCK, KEY_BLOCK))

    def bias_block(kb, carry):
        bias_scr[kb] = jnp.where(key_scr[kb] >= thr_b, 0.0, NEG_BIG).astype(F32)
        return carry

    lax.fori_loop(0, nkb, bias_block, 0)

    def masked_logits(kb, hk):
        k0 = pl.multiple_of(kb * KEY_BLOCK, KEY_BLOCK)
        kblk = kr_ref[pl.ds(k0, KEY_BLOCK), hk * HEAD_DIM:(hk + 1) * HEAD_DIM]
        s = lax.dot_general(qs_ref[0, hk], kblk, (((1,), (1,)), ((), ())), preferred_element_type=F32)
        return s + jnp.concatenate([bias_scr[kb]] * GQA, axis=0)

    m_scr[...] = jnp.full(m_scr.shape, NEG_BIG, F32)

    def max_block(kb):
        for hk in range(N_KV_HEADS):
            s = masked_logits(kb, hk)
            part = s[:, 0:LANES]
            for c in range(1, nsub):
                part = jnp.maximum(part, s[:, c * LANES:(c + 1) * LANES])
            m_scr[hk] = jnp.maximum(m_scr[hk], part)

    sweep(max_block, ATTN_UNROLL)
    for hk in range(N_KV_HEADS):
        row_max = jnp.max(m_scr[hk], axis=1, keepdims=True)
        m_scr[hk] = jnp.broadcast_to(row_max, m_scr.shape[1:])
    l_scr[...] = jnp.zeros(l_scr.shape, F32)
    acc_scr[...] = jnp.zeros(acc_scr.shape, F32)

    def pv_block(kb):
        k0 = pl.multiple_of(kb * KEY_BLOCK, KEY_BLOCK)
        for hk in range(N_KV_HEADS):
            vblk = v_ref[pl.ds(k0, KEY_BLOCK), hk * HEAD_DIM:(hk + 1) * HEAD_DIM]
            p = jnp.exp(masked_logits(kb, hk) - jnp.concatenate([m_scr[hk]] * nsub, axis=1))
            part = p[:, 0:LANES]
            for c in range(1, nsub):
                part = part + p[:, c * LANES:(c + 1) * LANES]
            l_scr[hk] += part
            acc_scr[hk] += jnp.dot(p.astype(BF16), vblk, preferred_element_type=F32)

    sweep(pv_block, ATTN_UNROLL)
    for hk in range(N_KV_HEADS):
        out = acc_scr[hk] / jnp.sum(l_scr[hk], axis=1, keepdims=True)
        for g in range(GQA):
            h = hk * GQA + g
            o_ref[:, h * HEAD_DIM:(h + 1) * HEAD_DIM] = out[g * Q_BLOCK:(g + 1) * Q_BLOCK].astype(o_ref.dtype)


def _dsa(qs, qis, small, kid, kr, proj, offs, bsz, s_len, wi_off):
    nblk = s_len // Q_BLOCK
    nkb = s_len // KEY_BLOCK
    topk = min(MAX_TOPK, s_len // 4)
    vcol = offs["v"] // D_KV
    return pl.pallas_call(
        functools.partial(_dsa_kernel, topk=topk, wi_off=wi_off),
        grid=(bsz, nblk),
        in_specs=[pl.BlockSpec((1, N_KV_HEADS, GQA * Q_BLOCK, HEAD_DIM), lambda b, i: (b * nblk + i, 0, 0, 0)),
                  pl.BlockSpec((1, IDX_HEADS, Q_BLOCK, LANES), lambda b, i: (b * nblk + i, 0, 0, 0)),
                  pl.BlockSpec((Q_BLOCK, small.shape[1]), lambda b, i: (b * nblk + i, 0)),
                  pl.BlockSpec((s_len, LANES), lambda b, i: (b, 0)),
                  pl.BlockSpec((s_len, D_KV), lambda b, i: (b, 0)),
                  pl.BlockSpec((s_len, D_KV), lambda b, i: (b, vcol))],
        out_specs=pl.BlockSpec((Q_BLOCK, D_ATTN), lambda b, i: (b * nblk + i, 0)),
        out_shape=jax.ShapeDtypeStruct((bsz * s_len, D_ATTN), BF16),
        scratch_shapes=[pltpu.VMEM((IDX_HEADS, Q_BLOCK, LANES), F32),
                        pltpu.VMEM((nkb, Q_BLOCK, KEY_BLOCK), jnp.int32),
                        pltpu.VMEM((nkb, Q_BLOCK, KEY_BLOCK), F32),
                        pltpu.VMEM((N_KV_HEADS, GQA * Q_BLOCK, LANES), F32),
                        pltpu.VMEM((N_KV_HEADS, GQA * Q_BLOCK, LANES), F32),
                        pltpu.VMEM((N_KV_HEADS, GQA * Q_BLOCK, HEAD_DIM), F32)],
        compiler_params=_cparams(("parallel", "arbitrary")),
        name="dsa",
    )(qs, qis, small, kid, kr, proj)


def _conv_kernel(x_ref, halo_ref, w_ref, b_ref, o_ref, *, tiles_per_seq):
    st = x_ref.shape[0]
    first = (pl.program_id(0) % tiles_per_seq) == 0
    halo = jnp.where(first, 0.0, halo_ref[...].astype(F32))
    xx = jnp.concatenate([halo, x_ref[...].astype(F32)], axis=0)
    acc = b_ref[...] + w_ref[CONV_WIDTH - 1:CONV_WIDTH, :] * xx[SUBLANES:, :]
    for k in range(CONV_WIDTH - 1):
        shifted = pltpu.roll(xx, CONV_WIDTH - 1 - k, 0)[SUBLANES:, :]
        acc = acc + w_ref[k:k + 1, :] * shifted
    o_ref[...] = (acc * jax.nn.sigmoid(acc)).astype(o_ref.dtype)


def _conv(proj, xs_off, conv_w, conv_b, s_len):
    m = proj.shape[0]
    ch = conv_w.shape[1]
    st, ct = _pick(s_len, 512), _pick(ch, 512)
    c0 = xs_off // ct
    hb = st // SUBLANES
    return pl.pallas_call(
        functools.partial(_conv_kernel, tiles_per_seq=s_len // st),
        grid=(m // st, ch // ct),
        in_specs=[pl.BlockSpec((st, ct), lambda r, c: (r, c0 + c)),
                  pl.BlockSpec((SUBLANES, ct), lambda r, c: (jnp.maximum(r * hb - 1, 0), c0 + c)),
                  pl.BlockSpec((CONV_WIDTH, ct), lambda r, c: (0, c)),
                  pl.BlockSpec((1, ct), lambda r, c: (0, c))],
        out_specs=pl.BlockSpec((st, ct), lambda r, c: (r, c)),
        out_shape=jax.ShapeDtypeStruct((m, ch), BF16),
        compiler_params=_cparams(("parallel", "parallel")),
        name="conv",
    )(proj, proj, conv_w, conv_b.reshape(1, ch))


def _softplus(x):
    return jnp.maximum(x, 0.0) + jnp.log1p(jnp.exp(-jnp.abs(x)))


def _split_bf16(v, parts):
    out = []
    for _ in range(parts):
        p = v.astype(BF16)
        out.append(p)
        v = v - p.astype(F32)
    return out


def _ssd_kernel(x_ref, b_ref, c_ref, z_ref, dtc_ref, pc_ref, dsk_ref, ng_ref, o_ref, state_scr, *, heads):
    gw = heads * SSD_HEAD_DIM
    rep = LANES // heads
    cl = SSD_CHUNK

    @pl.when(pl.program_id(2) == 0)
    def _():
        state_scr[...] = jnp.zeros(state_scr.shape, F32)

    li = lax.broadcasted_iota(jnp.int32, (cl, cl), 0)
    si = lax.broadcasted_iota(jnp.int32, (cl, cl), 1)
    tri = li >= si
    tri_b = tri.astype(BF16)
    er = lax.broadcasted_iota(jnp.int32, (LANES, gw), 0)
    ec = lax.broadcasted_iota(jnp.int32, (LANES, gw), 1)
    expand = (er == lax.shift_right_logical(ec, SSD_HEAD_DIM.bit_length() - 1) * rep).astype(BF16)
    lane = lax.broadcasted_iota(jnp.int32, (cl, LANES), 1)
    neg_a = -jnp.exp(pc_ref[0, 1:2, :])

    state = state_scr[...]
    for sc in range(x_ref.shape[0] // cl):
        rows = slice(sc * cl, (sc + 1) * cl)
        dtc = _softplus(dtc_ref[0, 0, rows, :] + pc_ref[0, 0:1, :])
        adt_c = dtc * neg_a
        acs_c = sum(jnp.dot(tri_b, p, preferred_element_type=F32) for p in _split_bf16(adt_c, 3))
        acs_r = acs_c.T
        last = acs_c[cl - 1:cl, :]
        dec_hi, dec_lo = _split_bf16(jnp.exp(acs_c), 2)
        stacked = jnp.concatenate([dtc.astype(BF16), (dtc * jnp.exp(last - acs_c)).astype(BF16), dec_hi, dec_lo],
                                  axis=0)
        ex = jnp.dot(stacked, expand, preferred_element_type=F32)
        dt_e, dts_e, dec_e = ex[0:cl], ex[cl:2 * cl], ex[2 * cl:3 * cl] + ex[3 * cl:4 * cl]

        xg = x_ref[rows, :].astype(F32)
        xdt = (xg * dt_e).astype(BF16)
        xst = (xg * dts_e).astype(BF16)
        bm = b_ref[rows, :]
        cm = c_ref[rows, :]
        cb = lax.dot_general(cm, bm, (((1,), (1,)), ((), ())), preferred_element_type=F32)
        pieces = []
        for j in range(gw // LANES):
            ys = []
            for r in (2 * j, 2 * j + 1):
                col = jnp.broadcast_to(acs_c[:, r * rep:r * rep + 1], (cl, cl))
                row = jnp.broadcast_to(acs_r[r * rep:r * rep + 1, :], (cl, cl))
                lmat = jnp.where(tri, jnp.exp(col - row), 0.0)
                ys.append(jnp.dot((cb * lmat).astype(BF16), xdt[:, j * LANES:(j + 1) * LANES],
                                  preferred_element_type=F32))
            pieces.append(jnp.where(lane < SSD_HEAD_DIM, ys[0], ys[1]))
        y_diag = jnp.concatenate(pieces, axis=1) if len(pieces) > 1 else pieces[0]

        y_off = jnp.dot(cm, state.astype(BF16), preferred_element_type=F32) * dec_e
        y = y_diag + y_off + xg * dsk_ref[0]
        state = state * dec_e[cl - 1:cl, :] + lax.dot_general(
            bm, xst, (((0,), (0,)), ((), ())), preferred_element_type=F32)

        z = z_ref[rows, :].astype(F32)
        yz = y * (z * jax.nn.sigmoid(z))
        o_ref[rows, :] = _rms(yz, ng_ref[...]).astype(o_ref.dtype)
    state_scr[...] = state


def _ssd(conv_out, proj, small, z_off, dt_off, dt_bias, a_log, d_skip, norm_g, bsz, s_len):
    d_ssd = norm_g.shape[0]
    n_heads = a_log.shape[0]
    heads = n_heads // SSD_GROUPS
    gw = heads * SSD_HEAD_DIM
    rep = LANES // heads
    m = bsz * s_len
    dt = small[:, dt_off:dt_off + n_heads].reshape(bsz, s_len, SSD_GROUPS, heads)
    dtc = jnp.transpose(jnp.repeat(dt, rep, axis=-1), (0, 2, 1, 3))
    par = jnp.repeat(jnp.stack([dt_bias, a_log]).reshape(2, SSD_GROUPS, heads), rep, axis=-1)
    pc = jnp.transpose(par, (1, 0, 2))
    dsk = jnp.repeat(d_skip, SSD_HEAD_DIM).reshape(SSD_GROUPS, 1, gw)
    b0 = d_ssd // SSD_STATE
    c0 = b0 + SSD_GROUPS
    z0 = z_off // gw
    rows = SSD_STEP_CHUNKS * SSD_CHUNK
    assert s_len % rows == 0
    nc = s_len // rows
    return pl.pallas_call(
        functools.partial(_ssd_kernel, heads=heads),
        grid=(bsz, SSD_GROUPS, nc),
        in_specs=[pl.BlockSpec((rows, gw), lambda b, g, c: (b * nc + c, g)),
                  pl.BlockSpec((rows, SSD_STATE), lambda b, g, c: (b * nc + c, b0 + g)),
                  pl.BlockSpec((rows, SSD_STATE), lambda b, g, c: (b * nc + c, c0 + g)),
                  pl.BlockSpec((rows, gw), lambda b, g, c: (b * nc + c, z0 + g)),
                  pl.BlockSpec((1, 1, rows, LANES), lambda b, g, c: (b, g, c, 0)),
                  pl.BlockSpec((1, 2, LANES), lambda b, g, c: (g, 0, 0)),
                  pl.BlockSpec((1, 1, gw), lambda b, g, c: (g, 0, 0)),
                  pl.BlockSpec((1, gw), lambda b, g, c: (0, g))],
        out_specs=pl.BlockSpec((rows, gw), lambda b, g, c: (b * nc + c, g)),
        out_shape=jax.ShapeDtypeStruct((m, d_ssd), BF16),
        scratch_shapes=[pltpu.VMEM((SSD_STATE, gw), F32)],
        compiler_params=_cparams(("parallel", "parallel", "arbitrary")),
        name="ssd",
    )(conv_out, conv_out, conv_out, proj, dtc, pc, dsk, norm_g.reshape(1, d_ssd))


def _ffn(hn, x, w13_stack, w2_stack, layer, g_post, g_next):
    h = _swiglu_up(hn, _layer_bf16(w13_stack, layer))
    mm = _matmul(h, _layer_bf16(w2_stack, layer), F32, name="ffn_down")
    return _post(mm, x, g_post, g_next, 0.5)


def _mixer(hn, x, pos, w_in_bf16_stack, idx_ln_g, idx_ln_b, conv_w, conv_b, dt_bias, a_log, d_skip, ssd_norm_g,
           wa_stack, ws_stack, wo_stack, layer, g_post, g_next, bsz, s_len):
    d = x.shape[1]
    d_ssd = ssd_norm_g.shape[0]
    n_ssd_heads = a_log.shape[0]
    gn = SSD_GROUPS * SSD_STATE
    sizes = (D_ATTN, D_KV, D_KV, D_IDX, IDX_DIM, IDX_HEADS, d_ssd, d_ssd, gn, gn, n_ssd_heads, d, d)
    names = ("q", "k", "v", "qi", "ki", "wi", "z", "xs", "bs", "cs", "dt", "ga", "gs")
    starts = {nm: int(s0) for nm, s0 in zip(names, np.concatenate([[0], np.cumsum(sizes)[:-1]]))}
    size = dict(zip(names, sizes))
    order = ("q", "qi", "k", "v", "z", "xs", "bs", "cs", "ga", "gs")
    offs, o = {}, 0
    for nm in order:
        offs[nm] = o
        o += size[nm]
    dt_off = 2 * IDX_DIM
    wi_off = dt_off + n_ssd_heads
    used = wi_off + IDX_HEADS
    n_small = -(-used // LANES) * LANES
    small_segments = ((starts["ki"], IDX_DIM, 0), (starts["ki"], IDX_DIM, IDX_DIM),
                      (starts["dt"], n_ssd_heads, dt_off), (starts["wi"], IDX_HEADS, wi_off))
    w_big, w_small = _w_in_relayout(w_in_bf16_stack, layer,
                                    tuple((starts[nm], size[nm], offs[nm]) for nm in order), o,
                                    small_segments, n_small)

    proj = _matmul(hn, w_big, BF16, name="in_proj")
    small = _matmul(hn, w_small, F32, name="in_proj_small")

    qs, qis, kr, kid = _attn_prep(pos, proj, small, offs, idx_ln_g, idx_ln_b)
    o_attn = _dsa(qs, qis, small, kid, kr, proj, offs, bsz, s_len, wi_off)

    conv_out = _conv(proj, offs["xs"], conv_w, conv_b, s_len)
    y = _ssd(conv_out, proj, small, offs["z"], dt_off, dt_bias, a_log, d_skip, ssd_norm_g, bsz, s_len)

    merged = _merge(o_attn, _layer_bf16(wa_stack, layer), y, _layer_bf16(ws_stack, layer), proj,
                    offs["ga"], offs["gs"], d)
    mm = _matmul(merged, _layer_bf16(wo_stack, layer), F32, name="out_proj")
    return _post(mm, x, g_post, g_next, 1.0)


def kernel(x, positions, ffn1_pre_g, ffn1_w13, ffn1_w2, ffn1_post_g, mix_pre_g, w_in, idx_ln_g, idx_ln_b, conv_w, conv_b, dt_bias, a_log, d_skip, ssd_norm_g, w_attn_o, w_ssd_o, w_out, mix_post_g, ffn2_pre_g, ffn2_w13, ffn2_w2, ffn2_post_g):
    bsz, s_len, d = x.shape
    depth = ffn1_w13.shape[0]
    assert s_len % KEY_BLOCK == 0 and s_len % SSD_CHUNK == 0 and d % LANES == 0
    m = bsz * s_len
    xf = x.reshape(m, d)
    pos = positions.reshape(m, 1)
    hn = _prenorm(xf, ffn1_pre_g[0])
    w_in = w_in.astype(BF16)
    for l in range(depth):
        xf, hn = _ffn(hn, xf, ffn1_w13, ffn1_w2, l, ffn1_post_g[l], mix_pre_g[l])
        xf, hn = _mixer(hn, xf, pos, w_in, idx_ln_g[l], idx_ln_b[l], conv_w[l], conv_b[l], dt_bias[l],
                        a_log[l], d_skip[l], ssd_norm_g[l], w_attn_o, w_ssd_o, w_out, l,
                        mix_post_g[l], ffn2_pre_g[l], bsz, s_len)
        g_next = ffn1_pre_g[l + 1] if l + 1 < depth else None
        xf, hn = _ffn(hn, xf, ffn2_w13, ffn2_w2, l, ffn2_post_g[l], g_next)
    return xf.reshape(bsz, s_len, d)
```
